```python
import math
import jax, jax.numpy as jnp
from jax import lax
import numpy as np

D_MODEL = 1024
BATCH = 8
SEQ = 2048
DEPTH = 2
DEC_BATCH = 128
DEC_SEQ = 1
PAST_LEN = 2048
PAGE_SIZE = 128

HEAD_DIM = 64
NSA_HEADS = 8
NSA_KV_HEADS = 2
NSA_GROUP = NSA_HEADS // NSA_KV_HEADS
CMP_BLOCK = 32
CMP_STRIDE = 16
CMP_HIDDEN = 2 * HEAD_DIM
SLC_BLOCK = 64
SLC_TOP = 3
WINDOW = 256
MOBA_HEADS = 8
MOBA_BLOCK = 256
MOBA_TOP = 3
DIFF_HEADS = D_MODEL // (2 * HEAD_DIM)
D_FF = 128 * ((8 * D_MODEL + 3 * 128 - 1) // (3 * 128))
N_EVEN = (DEPTH + 1) // 2
N_ODD = DEPTH // 2
EVEN_IN = NSA_HEADS * HEAD_DIM + 6 * NSA_KV_HEADS * HEAD_DIM + 3 * NSA_HEADS + 3 * MOBA_HEADS * HEAD_DIM
ODD_IN = 3 * DIFF_HEADS * 2 * HEAD_DIM
Q_BLOCK = 128
MOBA_Q_BLOCK = 16
RMS_EPS = 1e-6
NEG_INF = -1e30

kernel_name = "nsa_moba_diff_macaron_decoder_step"


def rms_norm(x, gain):
    xf = x.astype(jnp.float32)
    y = xf * lax.rsqrt(jnp.mean(xf * xf, axis=-1, keepdims=True) + RMS_EPS)
    return (y * gain.astype(jnp.float32)).astype(x.dtype)


def swiglu_ffn(x, norm_gain, w_in, w_out):
    gate, up = jnp.split(rms_norm(x, norm_gain) @ w_in, 2, axis=-1)
    return (jax.nn.silu(gate) * up) @ w_out


def alibi_slopes(n_heads):
    return jnp.exp2(-8.0 * jnp.arange(1, n_heads + 1, dtype=jnp.float32) / n_heads)


def masked_softmax(scores, mask):
    p = jax.nn.softmax(jnp.where(mask, scores, NEG_INF), axis=-1)
    return jnp.where(mask, p, 0.0)


def q_block_size(n_q, preferred):
    return preferred if n_q % preferred == 0 else n_q


def map_query_blocks(block_fn, n_q, qb):
    out = lax.map(block_fn, jnp.arange(n_q // qb))
    out = jnp.moveaxis(out, 0, 1)
    return out.reshape(out.shape[0], n_q, *out.shape[3:])


def gather_pages(pool, page_table):
    rows = pool[page_table]
    return rows.reshape(rows.shape[0], -1, *pool.shape[2:])


def pad_time(x, length):
    return jnp.pad(x, [(0, 0), (0, length - x.shape[1])] + [(0, 0)] * (x.ndim - 2))


def gather_blocks(blocks, idx):
    bi = jnp.arange(blocks.shape[0])[:, None, None, None]
    hi = jnp.arange(blocks.shape[1])[None, :, None, None]
    return blocks[bi, hi, idx]


def nsa_compress(rows, pos_emb, w1, w2):
    B, T, G, d = rows.shape
    ratio = CMP_BLOCK // CMP_STRIDE
    n_c = -(-T // CMP_STRIDE)
    chunks = pad_time(rows, (n_c + ratio - 1) * CMP_STRIDE).reshape(B, n_c + ratio - 1, CMP_STRIDE, G, d)
    blocks = jnp.concatenate([chunks[:, i:i + n_c] for i in range(ratio)], axis=2)
    blocks = blocks + pos_emb[:, None, :]
    flat = jnp.moveaxis(blocks, 3, 2).reshape(B, n_c, G, CMP_BLOCK * d)
    return jax.nn.silu(flat @ w1) @ w2


def nsa_attention(q, gates, k_cmp, v_cmp, k_slc, v_slc, k_win, v_win, q_pos0, win_pos0, prm):
    B, Tq, H, d = q.shape
    G, R = NSA_KV_HEADS, NSA_GROUP
    Tkv = k_slc.shape[1]
    scale = d ** -0.5
    slopes = alibi_slopes(NSA_HEADS).reshape(1, G, R, 1, 1)
    kc = rms_norm(nsa_compress(k_cmp, prm["cmp_k_pe"], prm["cmp_k_w1"], prm["cmp_k_w2"]), prm["nsa_kc_gain"])
    vc = nsa_compress(v_cmp, prm["cmp_v_pe"], prm["cmp_v_w1"], prm["cmp_v_w2"])
    n_c = kc.shape[1]
    c_start = jnp.arange(n_c) * CMP_STRIDE
    c_end = c_start + CMP_BLOCK - 1
    n_s = -(-Tkv // SLC_BLOCK)
    k_top = min(SLC_TOP, n_s)
    s_start = jnp.arange(n_s) * SLC_BLOCK
    cover = ((c_start[:, None] < s_start[None, :] + SLC_BLOCK) & (c_end[:, None] >= s_start[None, :])).astype(jnp.float32)
    ks_blk = jnp.moveaxis(pad_time(k_slc, n_s * SLC_BLOCK).reshape(B, n_s, SLC_BLOCK, G, d), 3, 1)
    vs_blk = jnp.moveaxis(pad_time(v_slc, n_s * SLC_BLOCK).reshape(B, n_s, SLC_BLOCK, G, d), 3, 1)
    kw = jnp.pad(k_win, ((0, 0), (WINDOW, 0), (0, 0), (0, 0)))
    vw = jnp.pad(v_win, ((0, 0), (WINDOW, 0), (0, 0), (0, 0)))
    qb = q_block_size(Tq, Q_BLOCK)

    def block(c):
        t0 = q_pos0 + c * qb
        t = t0 + jnp.arange(qb)
        qc = lax.dynamic_slice_in_dim(q, c * qb, qb, axis=1).reshape(B, qb, G, R, d)
        gc = jax.nn.sigmoid(lax.dynamic_slice_in_dim(gates, c * qb, qb, axis=1).astype(jnp.float32)).reshape(B, qb, G, R, 3)
        s = jnp.einsum("bqgrd,bcgd->bgrqc", qc, kc, preferred_element_type=jnp.float32) * scale
        s = s - slopes * (t[:, None] - c_end[None, :]).astype(jnp.float32)
        p_c = masked_softmax(s, c_end[None, :] <= t[:, None])
        o_c = jnp.einsum("bgrqc,bcgd->bqgrd", p_c.astype(vc.dtype), vc)
        own = t // SLC_BLOCK
        imp = jnp.einsum("bgqc,cn->bgqn", p_c.sum(axis=2), cover)
        imp = jnp.where(jnp.arange(n_s)[None, :] < own[:, None], imp, -1.0)
        _, top = lax.top_k(imp, k_top)
        idx = jnp.concatenate([top, jnp.broadcast_to(own[:, None], (B, G, qb, 1))], axis=-1)
        valid = jnp.concatenate([top < own[:, None], jnp.ones((B, G, qb, 1), bool)], axis=-1)
        ksel = gather_blocks(ks_blk, idx).reshape(B, G, qb, -1, d)
        vsel = gather_blocks(vs_blk, idx).reshape(B, G, qb, -1, d)
        pos = (idx[..., None] * SLC_BLOCK + jnp.arange(SLC_BLOCK)).reshape(B, G, qb, -1)
        mask = jnp.repeat(valid, SLC_BLOCK, axis=-1) & (pos <= t[:, None])
        s = jnp.einsum("bqgrd,bgqkd->bgrqk", qc, ksel, preferred_element_type=jnp.float32) * scale
        s = s - slopes * (t[:, None] - pos)[:, :, None].astype(jnp.float32)
        p_s = masked_softmax(s, mask[:, :, None])
        o_s = jnp.einsum("bgrqk,bgqkd->bqgrd", p_s.astype(vsel.dtype), vsel)
        kwc = lax.dynamic_slice_in_dim(kw, t0 - win_pos0, WINDOW + qb, axis=1)
        vwc = lax.dynamic_slice_in_dim(vw, t0 - win_pos0, WINDOW + qb, axis=1)
        wpos = t0 - WINDOW + jnp.arange(WINDOW + qb)
        dist = t[:, None] - wpos[None, :]
        mask = (wpos[None, :] >= win_pos0) & (dist >= 0) & (dist <= WINDOW)
        s = jnp.einsum("bqgrd,bkgd->bgrqk", qc, kwc, preferred_element_type=jnp.float32) * scale
        s = s - slopes * dist.astype(jnp.float32)
        p_w = masked_softmax(s, mask)
        o_w = jnp.einsum("bgrqk,bkgd->bqgrd", p_w.astype(vwc.dtype), vwc)
        o = gc[..., 0:1] * o_c + gc[..., 1:2] * o_s + gc[..., 2:3] * o_w
        return o.reshape(B, qb, H * d).astype(q.dtype)

    return map_query_blocks(block, Tq, qb)


def moba_attention(q, k, v, q_pos0):
    B, Tq, H, d = q.shape
    Tkv = k.shape[1]
    scale = d ** -0.5
    n_b = -(-Tkv // MOBA_BLOCK)
    k_top = min(MOBA_TOP, n_b)
    k_blk = jnp.moveaxis(pad_time(k, n_b * MOBA_BLOCK).reshape(B, n_b, MOBA_BLOCK, H, d), 3, 1)
    v_blk = jnp.moveaxis(pad_time(v, n_b * MOBA_BLOCK).reshape(B, n_b, MOBA_BLOCK, H, d), 3, 1)
    k_mean = jnp.mean(k_blk.astype(jnp.float32), axis=3)
    slopes = alibi_slopes(MOBA_HEADS)[None, :, None, None]
    qb = q_block_size(Tq, MOBA_Q_BLOCK)

    def block(c):
        t = q_pos0 + c * qb + jnp.arange(qb)
        qc = lax.dynamic_slice_in_dim(q, c * qb, qb, axis=1)
        own = t // MOBA_BLOCK
        route = jnp.einsum("bqhd,bhnd->bhqn", qc.astype(jnp.float32), k_mean)
        route = jnp.where(jnp.arange(n_b)[None, :] < own[:, None], route, NEG_INF)
        _, top = lax.top_k(route, k_top)
        idx = jnp.concatenate([top, jnp.broadcast_to(own[:, None], (B, H, qb, 1))], axis=-1)
        valid = jnp.concatenate([top < own[:, None], jnp.ones((B, H, qb, 1), bool)], axis=-1)
        kg = gather_blocks(k_blk, idx).reshape(B, H, qb, -1, d)
        vg = gather_blocks(v_blk, idx).reshape(B, H, qb, -1, d)
        pos = (idx[..., None] * MOBA_BLOCK + jnp.arange(MOBA_BLOCK)).reshape(B, H, qb, -1)
        mask = jnp.repeat(valid, MOBA_BLOCK, axis=-1) & (pos <= t[:, None])
        s = jnp.einsum("bqhd,bhqkd->bhqk", qc, kg, preferred_element_type=jnp.float32) * scale
        s = s - slopes * (t[:, None] - pos).astype(jnp.float32)
        p = masked_softmax(s, mask)
        o = jnp.einsum("bhqk,bhqkd->bqhd", p.astype(vg.dtype), vg)
        return o.reshape(B, qb, H * d)

    return map_query_blocks(block, Tq, qb)


def diff_attention(q, k, v, q_pos0, prm, lambda_init):
    B, Tq, H = q.shape[:3]
    d = q.shape[-1]
    Tkv = k.shape[1]
    scale = d ** -0.5
    lam = (jnp.exp(jnp.sum(prm["lq1"] * prm["lk1"], dtype=jnp.float32))
           - jnp.exp(jnp.sum(prm["lq2"] * prm["lk2"], dtype=jnp.float32)) + lambda_init)
    slopes = alibi_slopes(DIFF_HEADS)[None, :, None, None]
    kpos = jnp.arange(Tkv)
    qb = q_block_size(Tq, Q_BLOCK)

    def block(c):
        t = q_pos0 + c * qb + jnp.arange(qb)
        qc = lax.dynamic_slice_in_dim(q, c * qb, qb, axis=1)
        mask = kpos[None, :] <= t[:, None]
        bias = slopes * (t[:, None] - kpos[None, :]).astype(jnp.float32)
        s = jnp.einsum("bqhmd,bkhmd->mbhqk", qc, k, preferred_element_type=jnp.float32) * scale - bias
        p12 = masked_softmax(s, mask)
        attn = p12[0] - lam * p12[1]
        o = jnp.einsum("bhqk,bkhe->bqhe", attn.astype(v.dtype), v)
        o = rms_norm(o, prm["subln_gain"]) * (1.0 - lambda_init)
        return o.reshape(B, qb, H * 2 * d)

    return map_query_blocks(block, Tq, qb)


def even_mixer(h, prm, past):
    B, T, _ = h.shape
    d, G, HA, HB = HEAD_DIM, NSA_KV_HEADS, NSA_HEADS, MOBA_HEADS
    sizes = [HA * d, G * d, G * d, G * d, G * d, G * d, G * d, 3 * HA, HB * d, HB * d, HB * d]
    splits = np.cumsum(sizes)[:-1].tolist()
    q_n, kc, vc, ks, vs, kw, vw, gates, q_m, k_m, v_m = jnp.split(h @ prm["w_in"], splits, axis=-1)
    q_n = rms_norm(q_n.reshape(B, T, HA, d), prm["nsa_q_gain"])
    kc, vc = kc.reshape(B, T, G, d), vc.reshape(B, T, G, d)
    ks = rms_norm(ks.reshape(B, T, G, d), prm["nsa_ks_gain"])
    vs = vs.reshape(B, T, G, d)
    kw = rms_norm(kw.reshape(B, T, G, d), prm["nsa_kw_gain"])
    vw = vw.reshape(B, T, G, d)
    gates = gates.reshape(B, T, HA, 3)
    q_m = rms_norm(q_m.reshape(B, T, HB, d), prm["moba_q_gain"])
    k_m = rms_norm(k_m.reshape(B, T, HB, d), prm["moba_k_gain"])
    v_m = v_m.reshape(B, T, HB, d)
    new_rows = (kc, vc, ks, vs, k_m, v_m)
    if past is None:
        q_pos0, win_pos0, n_keep = 0, 0, min(WINDOW, T)
        kc_f, vc_f, ks_f, vs_f, km_f, vm_f = new_rows
        kw_all, vw_all = kw, vw
    else:
        pt = past["page_table"]
        q_pos0 = pt.shape[1] * past["cmp_k"].shape[1]
        caches = (past["cmp_k"], past["cmp_v"], past["slc_k"], past["slc_v"], past["moba_k"], past["moba_v"])
        kc_f, vc_f, ks_f, vs_f, km_f, vm_f = [jnp.concatenate([gather_pages(cache, pt), new], axis=1)
                                              for cache, new in zip(caches, new_rows)]
        n_keep = past["win_k"].shape[1]
        win_pos0 = q_pos0 - n_keep
        kw_all = jnp.concatenate([past["win_k"], kw], axis=1)
        vw_all = jnp.concatenate([past["win_v"], vw], axis=1)
    o_nsa = nsa_attention(q_n, gates, kc_f, vc_f, ks_f, vs_f, kw_all, vw_all, q_pos0, win_pos0, prm)
    o_moba = moba_attention(q_m, km_f, vm_f, q_pos0)
    y = jnp.concatenate([o_nsa, o_moba], axis=-1) @ prm["w_out"]
    state = (kc, vc, ks, vs, k_m, v_m, kw_all[:, -n_keep:], vw_all[:, -n_keep:])
    return y, state


def odd_mixer(h, prm, past, lambda_init):
    B, T, _ = h.shape
    d, H = HEAD_DIM, DIFF_HEADS
    q, k, v = jnp.split(h @ prm["w_in"], 3, axis=-1)
    q = rms_norm(q.reshape(B, T, H, 2, d), prm["q_gain"])
    k = rms_norm(k.reshape(B, T, H, 2, d), prm["k_gain"])
    k_rows = k.reshape(B, T, H, 2 * d)
    v = v.reshape(B, T, H, 2 * d)
    if past is None:
        q_pos0, k_f, v_f = 0, k_rows, v
    else:
        pt = past["page_table"]
        q_pos0 = pt.shape[1] * past["diff_k"].shape[1]
        k_f = jnp.concatenate([gather_pages(past["diff_k"], pt), k_rows], axis=1)
        v_f = jnp.concatenate([gather_pages(past["diff_v"], pt), v], axis=1)
    o = diff_attention(q, k_f.reshape(B, -1, H, 2, d), v_f, q_pos0, prm, lambda_init)
    return o @ prm["w_out"], (k_rows, v)


def run_trunk(x, params, past):
    even_states, odd_states = [], []
    for layer in range(DEPTH):
        x = x + 0.5 * swiglu_ffn(x, params["ffn1_norm"][layer], params["ffn1_w_in"][layer], params["ffn1_w_out"][layer])
        h = rms_norm(x, params["mix_norm"][layer])
        i = layer // 2
        group = "even" if layer % 2 == 0 else "odd"
        prm = {name: w[i] for name, w in params[group].items()}
        layer_past = None if past is None else dict({name: c[i] for name, c in past[group].items()},
                                                     page_table=past["page_table"])
        if layer % 2 == 0:
            y, state = even_mixer(h, prm, layer_past)
            even_states.append(state)
        else:
            y, state = odd_mixer(h, prm, layer_past, 0.8 - 0.6 * math.exp(-0.3 * layer))
            odd_states.append(state)
        x = x + y
        x = x + 0.5 * swiglu_ffn(x, params["ffn2_norm"][layer], params["ffn2_w_in"][layer], params["ffn2_w_out"][layer])
    even_out = [jnp.stack(s) for s in zip(*even_states)]
    odd_out = [jnp.stack(s) for s in zip(*odd_states)]
    return x, even_out, odd_out


def setup_inputs(seed: int = 0) -> dict:
    key = jax.random.key(seed)
    ks = iter(jax.random.split(key, 64))

    def nrm(shape, scale):
        return scale * jax.random.normal(next(ks), shape, jnp.float32)

    def gain(shape):
        return 1.0 + nrm(shape, 0.05)

    d, G = HEAD_DIM, NSA_KV_HEADS
    n_pages = PAST_LEN // PAGE_SIZE
    n_pool = (5 * DEC_BATCH * n_pages + 3) // 4
    win_buf = min(WINDOW, PAST_LEN)
    page_table = jax.random.permutation(next(ks), n_pool)[:DEC_BATCH * n_pages].reshape(DEC_BATCH, n_pages).astype(jnp.int32)
    return {
        "x_prompt": nrm((BATCH, SEQ, D_MODEL), 1.0),
        "x_sample": nrm((DEC_BATCH, DEC_SEQ, D_MODEL), 1.0),
        "cache_nsa_cmp_k": nrm((N_EVEN, n_pool, PAGE_SIZE, G, d), 1.0),
        "cache_nsa_cmp_v": nrm((N_EVEN, n_pool, PAGE_SIZE, G, d), 1.0),
        "cache_nsa_slc_k": nrm((N_EVEN, n_pool, PAGE_SIZE, G, d), 1.0),
        "cache_nsa_slc_v": nrm((N_EVEN, n_pool, PAGE_SIZE, G, d), 1.0),
        "cache_moba_k": nrm((N_EVEN, n_pool, PAGE_SIZE, MOBA_HEADS, d), 1.0),
        "cache_moba_v": nrm((N_EVEN, n_pool, PAGE_SIZE, MOBA_HEADS, d), 1.0),
        "state_nsa_win_k": nrm((N_EVEN, DEC_BATCH, win_buf, G, d), 1.0),
        "state_nsa_win_v": nrm((N_EVEN, DEC_BATCH, win_buf, G, d), 1.0),
        "cache_diff_k": nrm((N_ODD, n_pool, PAGE_SIZE, DIFF_HEADS, 2 * d), 1.0),
        "cache_diff_v": nrm((N_ODD, n_pool, PAGE_SIZE, DIFF_HEADS, 2 * d), 1.0),
        "page_table": page_table,
        "ffn1_norm": gain((DEPTH, D_MODEL)),
        "ffn1_w_in": nrm((DEPTH, D_MODEL, 2 * D_FF), D_MODEL ** -0.5),
        "ffn1_w_out": nrm((DEPTH, D_FF, D_MODEL), D_FF ** -0.5),
        "mix_norm": gain((DEPTH, D_MODEL)),
        "ffn2_norm": gain((DEPTH, D_MODEL)),
        "ffn2_w_in": nrm((DEPTH, D_MODEL, 2 * D_FF), D_MODEL ** -0.5),
        "ffn2_w_out": nrm((DEPTH, D_FF, D_MODEL), D_FF ** -0.5),
        "even_w_in": nrm((N_EVEN, D_MODEL, EVEN_IN), D_MODEL ** -0.5),
        "even_w_out": nrm((N_EVEN, D_MODEL, D_MODEL), D_MODEL ** -0.5),
        "nsa_q_gain": gain((N_EVEN, d)),
        "nsa_kc_gain": gain((N_EVEN, d)),
        "nsa_ks_gain": gain((N_EVEN, d)),
        "nsa_kw_gain": gain((N_EVEN, d)),
        "cmp_k_pe": nrm((N_EVEN, CMP_BLOCK, d), 0.1),
        "cmp_k_w1": nrm((N_EVEN, CMP_BLOCK * d, CMP_HIDDEN), (CMP_BLOCK * d) ** -0.5),
        "cmp_k_w2": nrm((N_EVEN, CMP_HIDDEN, d), CMP_HIDDEN ** -0.5),
        "cmp_v_pe": nrm((N_EVEN, CMP_BLOCK, d), 0.1),
        "cmp_v_w1": nrm((N_EVEN, CMP_BLOCK * d, CMP_HIDDEN), (CMP_BLOCK * d) ** -0.5),
        "cmp_v_w2": nrm((N_EVEN, CMP_HIDDEN, d), CMP_HIDDEN ** -0.5),
        "moba_q_gain": gain((N_EVEN, d)),
        "moba_k_gain": gain((N_EVEN, d)),
        "odd_w_in": nrm((N_ODD, D_MODEL, ODD_IN), D_MODEL ** -0.5),
        "odd_w_out": nrm((N_ODD, D_MODEL, D_MODEL), D_MODEL ** -0.5),
        "diff_q_gain": gain((N_ODD, d)),
        "diff_k_gain": gain((N_ODD, d)),
        "diff_lq1": nrm((N_ODD, d), 0.1),
        "diff_lk1": nrm((N_ODD, d), 0.1),
        "diff_lq2": nrm((N_ODD, d), 0.1),
        "diff_lk2": nrm((N_ODD, d), 0.1),
        "diff_subln_gain": gain((N_ODD, 2 * d)),
    }


def reference(x_prompt, x_sample, cache_nsa_cmp_k, cache_nsa_cmp_v, cache_nsa_slc_k, cache_nsa_slc_v,
              cache_moba_k, cache_moba_v, state_nsa_win_k, state_nsa_win_v, cache_diff_k, cache_diff_v,
              page_table, ffn1_norm, ffn1_w_in, ffn1_w_out, mix_norm, ffn2_norm, ffn2_w_in, ffn2_w_out,
              even_w_in, even_w_out, nsa_q_gain, nsa_kc_gain, nsa_ks_gain, nsa_kw_gain,
              cmp_k_pe, cmp_k_w1, cmp_k_w2, cmp_v_pe, cmp_v_w1, cmp_v_w2, moba_q_gain, moba_k_gain,
              odd_w_in, odd_w_out, diff_q_gain, diff_k_gain, diff_lq1, diff_lk1, diff_lq2, diff_lk2,
              diff_subln_gain):
    params = {
        "ffn1_norm": ffn1_norm, "ffn1_w_in": ffn1_w_in, "ffn1_w_out": ffn1_w_out,
        "mix_norm": mix_norm,
        "ffn2_norm": ffn2_norm, "ffn2_w_in": ffn2_w_in, "ffn2_w_out": ffn2_w_out,
        "even": {"w_in": even_w_in, "w_out": even_w_out, "nsa_q_gain": nsa_q_gain, "nsa_kc_gain": nsa_kc_gain,
                 "nsa_ks_gain": nsa_ks_gain, "nsa_kw_gain": nsa_kw_gain, "cmp_k_pe": cmp_k_pe,
                 "cmp_k_w1": cmp_k_w1, "cmp_k_w2": cmp_k_w2, "cmp_v_pe": cmp_v_pe, "cmp_v_w1": cmp_v_w1,
                 "cmp_v_w2": cmp_v_w2, "moba_q_gain": moba_q_gain, "moba_k_gain": moba_k_gain},
        "odd": {"w_in": odd_w_in, "w_out": odd_w_out, "q_gain": diff_q_gain, "k_gain": diff_k_gain,
                "lq1": diff_lq1, "lk1": diff_lk1, "lq2": diff_lq2, "lk2": diff_lk2, "subln_gain": diff_subln_gain},
    }
    past = {
        "page_table": page_table,
        "even": {"cmp_k": cache_nsa_cmp_k, "cmp_v": cache_nsa_cmp_v, "slc_k": cache_nsa_slc_k,
                 "slc_v": cache_nsa_slc_v, "moba_k": cache_moba_k, "moba_v": cache_moba_v,
                 "win_k": state_nsa_win_k, "win_v": state_nsa_win_v},
        "odd": {"diff_k": cache_diff_k, "diff_v": cache_diff_v},
    }
    y_prompt, even_p, odd_p = run_trunk(x_prompt, params, None)
    y_sample, even_s, odd_s = run_trunk(x_sample, params, past)
    p_ck, p_cv, p_sk, p_sv, p_mk, p_mv, p_wk, p_wv = even_p
    s_ck, s_cv, s_sk, s_sv, s_mk, s_mv, s_wk, s_wv = even_s
    p_dk, p_dv = odd_p
    s_dk, s_dv = odd_s
    return (y_prompt, y_sample, p_ck, s_ck, p_cv, s_cv, p_sk, s_sk, p_sv, s_sv, p_mk, s_mk, p_mv, s_mv,
            p_wk, s_wk, p_wv, s_wv, p_dk, s_dk, p_dv, s_dv)
```

```python
import functools
import math

import numpy as np
import jax
import jax.numpy as jnp
from jax import lax
from jax.experimental import pallas as pl
from jax.experimental.pallas import tpu as pltpu

F32 = jnp.float32
BF16 = jnp.bfloat16
HIGHEST = lax.Precision.HIGHEST

D_MODEL = 1024
HEAD_DIM = 64
NSA_HEADS = 8
NSA_KV_HEADS = 2
NSA_GROUP = NSA_HEADS // NSA_KV_HEADS
CMP_BLOCK = 32
CMP_STRIDE = 16
CMP_HIDDEN = 2 * HEAD_DIM
SLC_BLOCK = 64
SLC_TOP = 3
WINDOW = 256
MOBA_HEADS = 8
MOBA_BLOCK = 256
MOBA_TOP = 3
DIFF_HEADS = D_MODEL // (2 * HEAD_DIM)
PAGE_SIZE = 128
RMS_EPS = 1e-6
NEG_INF = -1e30
QK_SCALE = HEAD_DIM ** -0.5
LANES = 128
MXU_N = 256
VMEM_LIMIT = 56 * 1024 * 1024

E_QN, E_KS, E_KW, E_QM, E_KM = 0, 512, 640, 768, 1280
E_KC, E_VC, E_VS, E_VW, E_VM, E_GT = 1792, 1920, 2048, 2176, 2304, 2816
E_COLS = 3072
E_FLAGS = (1, 1, 1, 1, 1, 1, 1, 0, 0, 0, 0, 0)
O_Q, O_K, O_V = 0, 1024, 2048
O_COLS = 3072
O_FLAGS = (1, 1, 1, 1, 1, 1, 1, 1, 0, 0, 0, 0)


def _nt_dot(a, b, precision=None):
    return lax.dot_general(a, b, (((1,), (1,)), ((), ())), precision=precision,
                           preferred_element_type=F32)


def _dot(a, b, precision=None):
    return jnp.dot(a, b, precision=precision, preferred_element_type=F32)


def _mm(a, w):
    if w.dtype == F32:
        return _dot(a.astype(F32), w, precision=HIGHEST)
    return _dot(a.astype(w.dtype), w)


def _rms(x, gain):
    return x * lax.rsqrt(jnp.mean(x * x, axis=-1, keepdims=True) + RMS_EPS) * gain


def _params(*sem):
    return pltpu.CompilerParams(dimension_semantics=sem, vmem_limit_bytes=VMEM_LIMIT)


def _ffn_kernel(x_ref, g_ref, wg_ref, wu_ref, wo_ref, o_ref, xn_ref, acc_ref):
    j = pl.program_id(1)

    @pl.when(j == 0)
    def _():
        xn_ref[...] = _rms(x_ref[...], g_ref[...]).astype(xn_ref.dtype)
        acc_ref[...] = jnp.zeros_like(acc_ref)

    xn = xn_ref[...]
    gate = _mm(xn, wg_ref[...])
    up = _mm(xn, wu_ref[...])
    acc_ref[...] += _mm(gate * jax.nn.sigmoid(gate) * up, wo_ref[...])

    @pl.when(j == pl.num_programs(1) - 1)
    def _():
        o_ref[...] = x_ref[...] + 0.5 * acc_ref[...]


def _ffn(x, gain, w_in, w_out):
    m, d = x.shape
    f = w_out.shape[0]
    tm = min(m, 512)
    tf = f // 2
    nf = f // tf
    return pl.pallas_call(
        _ffn_kernel,
        grid=(m // tm, nf),
        in_specs=[
            pl.BlockSpec((tm, d), lambda i, j: (i, 0)),
            pl.BlockSpec((1, d), lambda i, j: (0, 0)),
            pl.BlockSpec((d, tf), lambda i, j: (0, j)),
            pl.BlockSpec((d, tf), lambda i, j: (0, j + nf)),
            pl.BlockSpec((tf, d), lambda i, j: (j, 0)),
        ],
        out_specs=pl.BlockSpec((tm, d), lambda i, j: (i, 0)),
        out_shape=jax.ShapeDtypeStruct((m, d), F32),
        scratch_shapes=[pltpu.VMEM((tm, d), w_in.dtype), pltpu.VMEM((tm, d), F32)],
        compiler_params=_params("parallel", "arbitrary"),
        name="ffn",
    )(x, gain.reshape(1, d), w_in, w_in, w_out)


def _proj_kernel(flag_ref, x_ref, g_ref, w_ref, cg_ref, bd_ref, o_ref, xn_ref):
    j = pl.program_id(1)

    @pl.when(j == 0)
    def _():
        xn_ref[...] = _rms(x_ref[...], g_ref[...]).astype(xn_ref.dtype)

    y = _mm(xn_ref[...], w_ref[...])

    @pl.when(flag_ref[j] == 1)
    def _():
        y2 = y * y
        hi = y2.astype(BF16)
        lo = (y2 - hi.astype(F32)).astype(BF16)
        ss = _dot(hi, bd_ref[...]) + _dot(lo, bd_ref[...])
        o_ref[...] = y * lax.rsqrt(ss * (1.0 / HEAD_DIM) + RMS_EPS) * cg_ref[...]

    @pl.when(flag_ref[j] == 0)
    def _():
        o_ref[...] = y


def _proj(x, gain, w, col_gain, flags):
    m, d = x.shape
    n = w.shape[1]
    tm = min(m, 512)
    tn = MXU_N
    head = np.arange(tn) // HEAD_DIM
    bd = jnp.asarray(head[:, None] == head[None, :], BF16)
    return pl.pallas_call(
        _proj_kernel,
        grid_spec=pltpu.PrefetchScalarGridSpec(
            num_scalar_prefetch=1,
            grid=(m // tm, n // tn),
            in_specs=[
                pl.BlockSpec((tm, d), lambda i, j, fl: (i, 0)),
                pl.BlockSpec((1, d), lambda i, j, fl: (0, 0)),
                pl.BlockSpec((d, tn), lambda i, j, fl: (0, j)),
                pl.BlockSpec((1, tn), lambda i, j, fl: (0, j)),
                pl.BlockSpec((tn, tn), lambda i, j, fl: (0, 0)),
            ],
            out_specs=pl.BlockSpec((tm, tn), lambda i, j, fl: (i, j)),
            scratch_shapes=[pltpu.VMEM((tm, d), w.dtype)],
        ),
        out_shape=jax.ShapeDtypeStruct((m, n), F32),
        compiler_params=_params("parallel", "arbitrary"),
        name="proj",
    )(jnp.asarray(flags, jnp.int32), x, gain.reshape(1, d), w, col_gain.reshape(1, n), bd)


def _oproj_kernel(n_in, x_ref, *refs):
    o_ref = refs[2 * n_in]
    acc = x_ref[...]
    for a_ref, w_ref in zip(refs[:n_in], refs[n_in:2 * n_in]):
        acc = acc + _dot(a_ref[...].astype(BF16), w_ref[...])
    o_ref[...] = acc


def _oproj(x, acts, ws):
    m, d = x.shape
    tm = min(m, 512)
    n_in = len(acts)
    in_specs = [pl.BlockSpec((tm, d), lambda i: (i, 0))]
    in_specs += [pl.BlockSpec((tm, a.shape[1]), lambda i: (i, 0)) for a in acts]
    in_specs += [pl.BlockSpec(w.shape, lambda i: (0, 0)) for w in ws]
    return pl.pallas_call(
        functools.partial(_oproj_kernel, n_in),
        grid=(m // tm,),
        in_specs=in_specs,
        out_specs=pl.BlockSpec((tm, d), lambda i: (i, 0)),
        out_shape=jax.ShapeDtypeStruct((m, d), F32),
        compiler_params=_params("parallel"),
        name="oproj",
    )(x, *acts, *ws)


def _top_select(cur, lane, own, n_top):
    sel = jnp.zeros(cur.shape, F32)
    for _ in range(n_top):
        mx = jnp.max(cur, axis=-1, keepdims=True)
        idx = jnp.min(jnp.where(cur == mx, lane, 4 * LANES), axis=-1, keepdims=True)
        pick = lane == idx
        sel = jnp.where(pick, jnp.where(lane < own, 1.0, 0.0), sel)
        cur = jnp.where(pick, -3e38, cur)
    return sel


def _softmax_update(carry, s, mask, v):
    m, l, acc = carry
    s = jnp.where(mask, s, NEG_INF)
    m_new = jnp.maximum(m, jnp.max(s, axis=-1, keepdims=True))
    alpha = jnp.exp(m - m_new)
    p = jnp.where(mask, jnp.exp(s - m_new), 0.0)
    l = alpha * l + jnp.sum(p, axis=-1, keepdims=True)
    acc = alpha * acc + _dot(p.astype(BF16), v)
    return m_new, l, acc


def _softmax_init(rows, width):
    return (jnp.full((rows, 1), NEG_INF, F32), jnp.zeros((rows, 1), F32), jnp.zeros((rows, width), F32))


def _safe_div(acc, l):
    return jnp.where(l > 0.0, acc / jnp.where(l > 0.0, l, 1.0), 0.0)


def _compress_rows(scr, n_c, w1_ref, w2_ref, pe_ref):
    x = jnp.concatenate([scr[pl.ds(r, n_c, stride=CMP_STRIDE), :] for r in range(CMP_BLOCK)], axis=1)
    hid = _dot(x.astype(BF16), w1_ref[...]) + _dot(pe_ref[...].astype(BF16), w1_ref[...])[0:1]
    hid = hid * jax.nn.sigmoid(hid)
    return _dot(hid.astype(BF16), w2_ref[...])


def _store_kc(kc, gk_ref, kc_ref):
    g = gk_ref[...]
    for grp in range(NSA_KV_HEADS):
        sl = slice(grp * HEAD_DIM, (grp + 1) * HEAD_DIM)
        kc_ref[0, :, sl] = _rms(kc[:, sl], g[:, sl])


def _cmp_prompt_kernel(k_ref, v_ref, w1k, w2k, pek, gk, w1v, w2v, pev, kc_ref, vc_ref, scr):
    t = k_ref.shape[0]
    n_c = t // CMP_STRIDE
    scr[t:, :] = jnp.zeros((scr.shape[0] - t, scr.shape[1]), F32)
    scr[:t, :] = k_ref[...]
    _store_kc(_compress_rows(scr, n_c, w1k, w2k, pek), gk, kc_ref)
    scr[:t, :] = v_ref[...]
    vc_ref[0] = _compress_rows(scr, n_c, w1v, w2v, pev)


def _cmp_decode_kernel(n_pages, pt_ref, *refs):
    k_pages = refs[:n_pages]
    v_pages = refs[n_pages:2 * n_pages]
    (w1k, w2k, pek, gk, w1v, w2v, pev, kc_ref, vc_ref, scr) = refs[2 * n_pages:]
    t = n_pages * PAGE_SIZE
    n_c = t // CMP_STRIDE
    scr[t:, :] = jnp.zeros((scr.shape[0] - t, scr.shape[1]), F32)
    for p, pg in enumerate(k_pages):
        scr[p * PAGE_SIZE:(p + 1) * PAGE_SIZE, :] = pg[0].T
    _store_kc(_compress_rows(scr, n_c, w1k, w2k, pek), gk, kc_ref)
    for p, pg in enumerate(v_pages):
        scr[p * PAGE_SIZE:(p + 1) * PAGE_SIZE, :] = pg[0].T
    vc_ref[0] = _compress_rows(scr, n_c, w1v, w2v, pev)


def _cmp_weights(pe, w1, w2):
    d, hdn, g_n = HEAD_DIM, CMP_HIDDEN, NSA_KV_HEADS
    w1r = w1.reshape(CMP_BLOCK, d, hdn)
    eye = jnp.eye(g_n, dtype=F32)
    w1b = jnp.einsum("rdk,gq->rgdqk", w1r, eye).reshape(CMP_BLOCK * g_n * d, g_n * hdn)
    w2b = jnp.einsum("kd,gq->gkqd", w2, eye).reshape(g_n * hdn, g_n * d)
    per = jnp.broadcast_to(pe.reshape(CMP_BLOCK, 1, d), (CMP_BLOCK, g_n, d)).reshape(1, -1)
    pe8 = jnp.concatenate([per, jnp.zeros((7, per.shape[1]), F32)], axis=0)
    return w1b.astype(BF16), w2b.astype(BF16), pe8


def _cmp_consts(wk, wv, kc_gain):
    gk = jnp.tile(kc_gain, NSA_KV_HEADS).reshape(1, NSA_KV_HEADS * HEAD_DIM)
    return [*wk, gk, *wv]


def _compress_prompt(p_even, wk, wv, kc_gain, b, t):
    n_c = t // CMP_STRIDE
    gd = NSA_KV_HEADS * HEAD_DIM
    consts = _cmp_consts(wk, wv, kc_gain)
    return pl.pallas_call(
        _cmp_prompt_kernel,
        grid=(b,),
        in_specs=[pl.BlockSpec((t, gd), lambda i: (i, E_KC // gd)), pl.BlockSpec((t, gd), lambda i: (i, E_VC // gd))]
        + [pl.BlockSpec(a.shape, lambda i: (0, 0)) for a in consts],
        out_specs=[pl.BlockSpec((1, n_c, gd), lambda i: (i, 0, 0))] * 2,
        out_shape=[jax.ShapeDtypeStruct((b, n_c, gd), F32)] * 2,
        scratch_shapes=[pltpu.VMEM((t + CMP_STRIDE, gd), F32)],
        compiler_params=_params("parallel"),
        name="nsa_compress_prompt",
    )(p_even, p_even, *consts)


def _compress_decode(page_table, k_pool_t, v_pool_t, wk, wv, kc_gain):
    b, n_pages = page_table.shape
    t = n_pages * PAGE_SIZE
    n_c = t // CMP_STRIDE
    gd = NSA_KV_HEADS * HEAD_DIM
    page_specs = [pl.BlockSpec((1, gd, PAGE_SIZE), lambda i, pt, p=p: (pt[i * n_pages + p], 0, 0))
                  for p in range(n_pages)]
    consts = _cmp_consts(wk, wv, kc_gain)
    return pl.pallas_call(
        functools.partial(_cmp_decode_kernel, n_pages),
        grid_spec=pltpu.PrefetchScalarGridSpec(
            num_scalar_prefetch=1,
            grid=(b,),
            in_specs=page_specs + page_specs + [pl.BlockSpec(a.shape, lambda i, pt: (0, 0)) for a in consts],
            out_specs=[pl.BlockSpec((1, n_c, gd), lambda i, pt: (i, 0, 0))] * 2,
            scratch_shapes=[pltpu.VMEM((t + CMP_STRIDE, gd), F32)],
        ),
        out_shape=[jax.ShapeDtypeStruct((b, n_c, gd), F32)] * 2,
        compiler_params=_params("parallel"),
        name="nsa_compress_decode",
    )(page_table.reshape(-1), *([k_pool_t] * n_pages), *([v_pool_t] * n_pages), *consts)


def _cover_matrix():
    j = np.arange(LANES)[:, None]
    n = np.arange(LANES)[None, :]
    c0 = j * CMP_STRIDE
    s0 = n * SLC_BLOCK
    return jnp.asarray((c0 < s0 + SLC_BLOCK) & (c0 + CMP_BLOCK - 1 >= s0), F32)


def _expand_matrix(block, t):
    n = np.arange(LANES)[:, None]
    s = np.arange(t)[None, :]
    return jnp.asarray(s // block == n, BF16)


def _nsa_kernel(q_ref, gt_ref, ksw_ref, vsw_ref, kc_ref, vc_ref, cov_ref, exp_ref, o_ref, selk_ref):
    i = pl.program_id(1)
    tq_n = q_ref.shape[0]
    r_n = NSA_GROUP
    d = HEAD_DIM
    t0 = i * tq_n
    rows = r_n * tq_n
    row = lax.broadcasted_iota(jnp.int32, (rows, 1), 0)
    tq = t0 + row % tq_n
    tq1 = tq[:tq_n]
    lane = lax.broadcasted_iota(jnp.int32, (1, LANES), 1)
    gates = jax.nn.sigmoid(gt_ref[:, :LANES])
    own = tq1 // SLC_BLOCK

    for g in range(NSA_KV_HEADS):
        kcol = slice(g * d, (g + 1) * d)
        wcol = slice(LANES + g * d, LANES + (g + 1) * d)
        qg = jnp.concatenate([q_ref[:, (g * r_n + r) * d:(g * r_n + r + 1) * d] for r in range(r_n)], axis=0)
        qg = (qg * QK_SCALE).astype(BF16)
        slope = jnp.exp2(-(g * r_n + row // tq_n + 1).astype(F32))

        c_end = lane * CMP_STRIDE + (CMP_BLOCK - 1)
        s = _nt_dot(qg, kc_ref[0, :, kcol].astype(BF16)) - slope * (tq - c_end).astype(F32)
        valid = c_end <= tq
        s = jnp.where(valid, s, NEG_INF)
        p_c = jnp.where(valid, jnp.exp(s - jnp.max(s, axis=-1, keepdims=True)), 0.0)
        p_c = _safe_div(p_c, jnp.sum(p_c, axis=-1, keepdims=True))
        o_c = _dot(p_c.astype(BF16), vc_ref[0, :, kcol].astype(BF16))

        p_sum = p_c[0:tq_n]
        for r in range(1, r_n):
            p_sum = p_sum + p_c[r * tq_n:(r + 1) * tq_n]
        imp = _dot(p_sum, cov_ref[...], precision=HIGHEST)
        cur = jnp.where(lane < own, imp, -1.0)
        sel = _top_select(cur, lane, own, SLC_TOP)
        sel = jnp.where(lane == own, 1.0, sel)
        selk_ref[...] = _dot(sel.astype(BF16), exp_ref[...])

        def slc_body(j, carry):
            k0 = pl.multiple_of(j * tq_n, tq_n)
            kt = ksw_ref[pl.ds(k0, tq_n), kcol].astype(BF16)
            vt = vsw_ref[pl.ds(k0, tq_n), kcol].astype(BF16)
            pos = k0 + lane
            s = _nt_dot(qg, kt) - slope * (tq - pos).astype(F32)
            mk = selk_ref[:, pl.ds(k0, tq_n)]
            mk = jnp.concatenate([mk] * r_n, axis=0) > 0.5
            return _softmax_update(carry, s, mk & (pos <= tq), vt)

        _, l_s, acc_s = lax.fori_loop(0, i + 1, slc_body, _softmax_init(rows, d))
        o_s = _safe_div(acc_s, l_s)

        carry = _softmax_init(rows, d)
        for jj in range(WINDOW // tq_n + 1):
            tile = i - WINDOW // tq_n + jj
            k0 = pl.multiple_of(jnp.maximum(tile, 0) * tq_n, tq_n)
            kt = ksw_ref[pl.ds(k0, tq_n), wcol].astype(BF16)
            vt = vsw_ref[pl.ds(k0, tq_n), wcol].astype(BF16)
            pos = tile * tq_n + lane
            dist = tq - pos
            s = _nt_dot(qg, kt) - slope * dist.astype(F32)
            carry = _softmax_update(carry, s, (pos >= 0) & (dist >= 0) & (dist <= WINDOW), vt)
        o_w = _safe_div(carry[2], carry[1])

        for r in range(r_n):
            h = g * r_n + r
            rs = slice(r * tq_n, (r + 1) * tq_n)
            o_ref[:, h * d:(h + 1) * d] = (gates[:, h:h + 1] * o_c[rs]
                                           + gates[:, NSA_HEADS + h:NSA_HEADS + h + 1] * o_s[rs]
                                           + gates[:, 2 * NSA_HEADS + h:2 * NSA_HEADS + h + 1] * o_w[rs])


def _nsa_prompt(p_even, kc, vc, b, t):
    tq_n = LANES
    nq = t // tq_n
    cov = _cover_matrix()
    exp = _expand_matrix(SLC_BLOCK, t)
    return pl.pallas_call(
        _nsa_kernel,
        grid=(b, nq),
        in_specs=[
            pl.BlockSpec((tq_n, NSA_HEADS * HEAD_DIM), lambda bi, i: (bi * nq + i, E_QN // 512)),
            pl.BlockSpec((tq_n, 2 * LANES), lambda bi, i: (bi * nq + i, E_GT // 256)),
            pl.BlockSpec((t, 2 * LANES), lambda bi, i: (bi, E_KS // 256)),
            pl.BlockSpec((t, 2 * LANES), lambda bi, i: (bi, E_VS // 256)),
            pl.BlockSpec((1, LANES, LANES), lambda bi, i: (bi, 0, 0)),
            pl.BlockSpec((1, LANES, LANES), lambda bi, i: (bi, 0, 0)),
            pl.BlockSpec((LANES, LANES), lambda bi, i: (0, 0)),
            pl.BlockSpec((LANES, t), lambda bi, i: (0, 0)),
        ],
        out_specs=pl.BlockSpec((tq_n, NSA_HEADS * HEAD_DIM), lambda bi, i: (bi * nq + i, 0)),
        out_shape=jax.ShapeDtypeStruct((b * t, NSA_HEADS * HEAD_DIM), F32),
        scratch_shapes=[pltpu.VMEM((tq_n, t), F32)],
        compiler_params=_params("parallel", "arbitrary"),
        name="nsa_prompt",
    )(p_even, p_even, p_even, p_even, kc, vc, cov, exp)


def _moba_kernel(q_ref, k_ref, v_ref, o_ref, km_ref):
    hp = pl.program_id(1)
    i = pl.program_id(2)
    tq_n = q_ref.shape[0]
    t = k_ref.shape[0]
    n_b = t // MOBA_BLOCK
    d = HEAD_DIM

    @pl.when(i == 0)
    def _():
        km_ref[...] = jnp.zeros_like(km_ref)
        for n in range(n_b):
            km_ref[n:n + 1, :] = jnp.mean(k_ref[n * MOBA_BLOCK:(n + 1) * MOBA_BLOCK, :], axis=0, keepdims=True)

    t0 = i * tq_n
    tq = t0 + lax.broadcasted_iota(jnp.int32, (tq_n, 1), 0)
    lane = lax.broadcasted_iota(jnp.int32, (1, LANES), 1)
    kpos = lax.broadcasted_iota(jnp.int32, (1, MOBA_BLOCK), 1)
    own = i

    for hh in range(2):
        col = slice(hh * d, (hh + 1) * d)
        hvec = jnp.zeros((1, 1), jnp.int32) + (hp * 2 + hh + 1)
        slope = jnp.exp2(-hvec.astype(F32))
        q = q_ref[:, col]
        route = _nt_dot(q, km_ref[:, col], precision=HIGHEST)
        cur = jnp.where(lane < own, route, NEG_INF)
        cur = jnp.where(lane < n_b, cur, -2e38)
        sel = _top_select(cur, lane, own, MOBA_TOP)
        sel = jnp.where(lane == own, 1.0, sel)
        qb = (q * QK_SCALE).astype(BF16)

        def body(n, carry):
            k0 = pl.multiple_of(n * MOBA_BLOCK, MOBA_BLOCK)
            kt = k_ref[pl.ds(k0, MOBA_BLOCK), col].astype(BF16)
            vt = v_ref[pl.ds(k0, MOBA_BLOCK), col].astype(BF16)
            pos = k0 + kpos
            s = _nt_dot(qb, kt) - slope * (tq - pos).astype(F32)
            chosen = jnp.sum(jnp.where(lane == n, sel, 0.0), axis=-1, keepdims=True) > 0.5
            return _softmax_update(carry, s, chosen & (pos <= tq), vt)

        _, l, acc = lax.fori_loop(0, i + 1, body, _softmax_init(tq_n, d))
        o_ref[:, col] = _safe_div(acc, l)


def _moba_prompt(p_even, b, t):
    tq_n = MOBA_BLOCK
    nq = t // tq_n
    return pl.pallas_call(
        _moba_kernel,
        grid=(b, MOBA_HEADS // 2, nq),
        in_specs=[
            pl.BlockSpec((tq_n, LANES), lambda bi, hp, i: (bi * nq + i, E_QM // LANES + hp)),
            pl.BlockSpec((t, LANES), lambda bi, hp, i: (bi, E_KM // LANES + hp)),
            pl.BlockSpec((t, LANES), lambda bi, hp, i: (bi, E_VM // LANES + hp)),
        ],
        out_specs=pl.BlockSpec((tq_n, LANES), lambda bi, hp, i: (bi * nq + i, hp)),
        out_shape=jax.ShapeDtypeStruct((b * t, MOBA_HEADS * HEAD_DIM), F32),
        scratch_shapes=[pltpu.VMEM((LANES, LANES), F32)],
        compiler_params=_params("parallel", "parallel", "arbitrary"),
        name="moba_prompt",
    )(p_even, p_even, p_even)


def _diff_lambda(lam_ref, lambda_init):
    lv = lam_ref[...]
    a = jnp.sum(lv[0:1] * lv[1:2], axis=-1, keepdims=True)
    c = jnp.sum(lv[2:3] * lv[3:4], axis=-1, keepdims=True)
    return jnp.exp(a) - jnp.exp(c) + lambda_init


def _diff_kernel(lambda_init, q_ref, k_ref, v_ref, lam_ref, sg_ref, o_ref):
    h = pl.program_id(1)
    i = pl.program_id(2)
    tq_n = q_ref.shape[0]
    d2 = 2 * HEAD_DIM
    t0 = i * tq_n
    row = lax.broadcasted_iota(jnp.int32, (2 * tq_n, 1), 0)
    tq = t0 + row % tq_n
    lane = lax.broadcasted_iota(jnp.int32, (1, d2), 1)
    kpos = lax.broadcasted_iota(jnp.int32, (1, tq_n), 1)
    slope = jnp.exp2(-(jnp.zeros((1, 1), jnp.int32) + (h + 1)).astype(F32))
    q = q_ref[...] * QK_SCALE
    q2 = jnp.concatenate([jnp.where(lane < HEAD_DIM, q, 0.0), jnp.where(lane >= HEAD_DIM, q, 0.0)], axis=0)
    q2 = q2.astype(BF16)

    def body(j, carry):
        k0 = pl.multiple_of(j * tq_n, tq_n)
        kt = k_ref[pl.ds(k0, tq_n), :].astype(BF16)
        vt = v_ref[pl.ds(k0, tq_n), :].astype(BF16)
        pos = k0 + kpos
        s = _nt_dot(q2, kt) - slope * (tq - pos).astype(F32)
        return _softmax_update(carry, s, pos <= tq, vt)

    _, l, acc = lax.fori_loop(0, i + 1, body, _softmax_init(2 * tq_n, d2))
    o = acc / l
    lam = _diff_lambda(lam_ref, lambda_init)
    attn = o[:tq_n] - lam * o[tq_n:]
    o_ref[...] = _rms(attn, sg_ref[...]) * (1.0 - lambda_init)


def _diff_prompt(p_odd, lam_rows, subln_gain, lambda_init, b, t):
    tq_n = 256
    nq = t // tq_n
    d2 = 2 * HEAD_DIM
    return pl.pallas_call(
        functools.partial(_diff_kernel, lambda_init),
        grid=(b, DIFF_HEADS, nq),
        in_specs=[
            pl.BlockSpec((tq_n, d2), lambda bi, h, i: (bi * nq + i, O_Q // d2 + h)),
            pl.BlockSpec((t, d2), lambda bi, h, i: (bi, O_K // d2 + h)),
            pl.BlockSpec((t, d2), lambda bi, h, i: (bi, O_V // d2 + h)),
            pl.BlockSpec(lam_rows.shape, lambda bi, h, i: (0, 0)),
            pl.BlockSpec((1, d2), lambda bi, h, i: (0, 0)),
        ],
        out_specs=pl.BlockSpec((tq_n, d2), lambda bi, h, i: (bi * nq + i, h)),
        out_shape=jax.ShapeDtypeStruct((b * t, DIFF_HEADS * d2), F32),
        compiler_params=_params("parallel", "parallel", "arbitrary"),
        name="diff_prompt",
    )(p_odd, p_odd, p_odd, lam_rows, subln_gain.reshape(1, d2))


def _decode_softmax(s, mask, s_new):
    sm = jnp.where(mask, s, NEG_INF)
    m = jnp.maximum(jnp.max(sm, axis=-1, keepdims=True), s_new)
    p = jnp.where(mask, jnp.exp(sm - m), 0.0)
    p_new = jnp.exp(s_new - m)
    l = jnp.sum(p, axis=-1, keepdims=True) + p_new
    return p, p_new, l


def _nsa_decode_kernel(n_pages, pt_ref, *refs):
    ks_pages = refs[:n_pages]
    vs_pages = refs[n_pages:2 * n_pages]
    (q_ref, gt_ref, kc_ref, vc_ref, wk_ref, wv_ref, new_ref, cov_ref, exp_ref, o_ref) = refs[2 * n_pages:]
    d = HEAD_DIM
    t = n_pages * PAGE_SIZE
    hrow = lax.broadcasted_iota(jnp.int32, (NSA_HEADS, 1), 0)
    grp0 = hrow < NSA_GROUP
    slope = jnp.exp2(-(hrow + 1).astype(F32))
    lane = lax.broadcasted_iota(jnp.int32, (1, LANES), 1)
    q64 = q_ref[0] * QK_SCALE
    q = jnp.where((lane >= d) == (hrow >= NSA_GROUP), jnp.concatenate([q64, q64], axis=1), 0.0)
    qb = q.astype(BF16)
    gates = jax.nn.sigmoid(gt_ref[0])
    new = new_ref[0]

    def own_group(o):
        return jnp.where(grp0, o[:, :d], o[:, d:])

    def new_row(r):
        return own_group(new[r:r + 1, :])

    c_end = lane * CMP_STRIDE + (CMP_BLOCK - 1)
    s = _nt_dot(qb, kc_ref[0].astype(BF16)) - slope * (t - c_end).astype(F32)
    valid = c_end <= t
    sm = jnp.where(valid, s, NEG_INF)
    m = jnp.max(sm, axis=-1, keepdims=True)
    p = jnp.where(valid, jnp.exp(sm - m), 0.0)
    p_c = p / jnp.sum(p, axis=-1, keepdims=True)
    o_c = own_group(_dot(p_c.astype(BF16), vc_ref[0].astype(BF16)))

    ps0 = jnp.sum(jnp.where(grp0, p_c, 0.0), axis=0, keepdims=True)
    ps1 = jnp.sum(jnp.where(grp0, 0.0, p_c), axis=0, keepdims=True)
    p_sum = jnp.where(grp0, ps0, ps1)
    imp = _dot(p_sum, cov_ref[...], precision=HIGHEST)
    own = t // SLC_BLOCK
    cur = jnp.where(lane < own, imp, -1.0)
    sel = _top_select(cur, lane, own, SLC_TOP)
    selk = _dot(sel.astype(BF16), exp_ref[...]) > 0.5

    ks = jnp.concatenate([pg[0] for pg in ks_pages], axis=1).astype(BF16)
    vs = jnp.concatenate([pg[0] for pg in vs_pages], axis=1).astype(BF16)
    pos = lax.broadcasted_iota(jnp.int32, (1, t), 1)
    s = _dot(qb, ks) - slope * (t - pos).astype(F32)
    s_new = jnp.sum(q * new[0:1, :], axis=-1, keepdims=True)
    p, p_new, l = _decode_softmax(s, selk, s_new)
    o_s = (own_group(_nt_dot(p.astype(BF16), vs)) + p_new * new_row(1)) / l

    wk = wk_ref[0].astype(BF16)
    wv = wv_ref[0].astype(BF16)
    w_n = wk.shape[1]
    wpos = (t - w_n) + lax.broadcasted_iota(jnp.int32, (1, w_n), 1)
    s = _dot(qb, wk) - slope * (t - wpos).astype(F32)
    s_new = jnp.sum(q * new[2:3, :], axis=-1, keepdims=True)
    p, p_new, l = _decode_softmax(s, (t - wpos) <= WINDOW, s_new)
    o_w = (own_group(_nt_dot(p.astype(BF16), wv)) + p_new * new_row(3)) / l

    o_ref[0] = gates[:, 0:1] * o_c + gates[:, 1:2] * o_s + gates[:, 2:3] * o_w


def _nsa_decode(page_table, ks_pool, vs_pool, q8, gates8, kc, vc, win_k, win_v, new_rows):
    b, n_pages = page_table.shape
    t = n_pages * PAGE_SIZE
    gd = NSA_KV_HEADS * HEAD_DIM
    page_specs = [pl.BlockSpec((1, gd, PAGE_SIZE), lambda i, pt, p=p: (pt[i * n_pages + p], 0, 0))
                  for p in range(n_pages)]
    per_b = lambda a: pl.BlockSpec((1,) + a.shape[1:], lambda i, pt: (i,) + (0,) * (a.ndim - 1))
    const = lambda a: pl.BlockSpec(a.shape, lambda i, pt: (0,) * a.ndim)
    cov = _cover_matrix()
    exp = _expand_matrix(SLC_BLOCK, t)
    per_sample = [q8, gates8, kc, vc, win_k, win_v, new_rows]
    return pl.pallas_call(
        functools.partial(_nsa_decode_kernel, n_pages),
        grid_spec=pltpu.PrefetchScalarGridSpec(
            num_scalar_prefetch=1,
            grid=(b,),
            in_specs=page_specs + page_specs + [per_b(a) for a in per_sample] + [const(cov), const(exp)],
            out_specs=pl.BlockSpec((1, NSA_HEADS, HEAD_DIM), lambda i, pt: (i, 0, 0)),
        ),
        out_shape=jax.ShapeDtypeStruct((b, NSA_HEADS, HEAD_DIM), F32),
        compiler_params=_params("parallel"),
        name="nsa_decode",
    )(page_table.reshape(-1), *([ks_pool] * n_pages), *([vs_pool] * n_pages), *per_sample, cov, exp)


def _moba_decode_kernel(n_pages, pt_ref, *refs):
    k_pages = refs[:n_pages]
    v_pages = refs[n_pages:2 * n_pages]
    q_ref, new_ref, exp_ref, o_ref = refs[2 * n_pages:]
    d = HEAD_DIM
    hd = MOBA_HEADS * d
    t = n_pages * PAGE_SIZE
    ppb = MOBA_BLOCK // PAGE_SIZE
    n_b = n_pages // ppb
    hrow = lax.broadcasted_iota(jnp.int32, (MOBA_HEADS, 1), 0)
    slope = jnp.exp2(-(hrow + 1).astype(F32))
    lane = lax.broadcasted_iota(jnp.int32, (1, LANES), 1)
    head_of_lane = lax.broadcasted_iota(jnp.int32, (1, hd), 1) // d
    own_head = head_of_lane == hrow
    q = jnp.where(own_head, q_ref[0], 0.0)
    new = new_ref[0]

    k_mean = jnp.zeros((hd, LANES), F32)
    for n in range(n_b):
        blk = k_pages[n * ppb][0]
        for pp in range(1, ppb):
            blk = blk + k_pages[n * ppb + pp][0]
        col = jnp.sum(blk, axis=-1, keepdims=True) * (1.0 / MOBA_BLOCK)
        k_mean = jnp.where(lane == n, col, k_mean)
    route = _dot(q, k_mean, precision=HIGHEST)
    cur = jnp.where(lane < n_b, route, -2e38)
    sel = _top_select(cur, lane, n_b, MOBA_TOP)
    selk = _dot(sel.astype(BF16), exp_ref[...]) > 0.5

    k_all = jnp.concatenate([pg[0] for pg in k_pages], axis=1).astype(BF16)
    v_all = jnp.concatenate([pg[0] for pg in v_pages], axis=1).astype(BF16)
    qs = q * QK_SCALE
    pos = lax.broadcasted_iota(jnp.int32, (1, t), 1)
    s = _dot(qs.astype(BF16), k_all) - slope * (t - pos).astype(F32)
    s_new = jnp.sum(qs * new[0:1], axis=-1, keepdims=True)
    p, p_new, l = _decode_softmax(s, selk, s_new)
    o = (_nt_dot(p.astype(BF16), v_all) + p_new * new[1:2]) / l
    o_ref[0] = jnp.sum(jnp.where(own_head, o, 0.0), axis=0, keepdims=True)


def _moba_decode(page_table, k_pool, v_pool, q_rows, new_rows):
    b, n_pages = page_table.shape
    t = n_pages * PAGE_SIZE
    hd = MOBA_HEADS * HEAD_DIM
    page_specs = [pl.BlockSpec((1, hd, PAGE_SIZE), lambda i, pt, p=p: (pt[i * n_pages + p], 0, 0))
                  for p in range(n_pages)]
    per_b = lambda a: pl.BlockSpec((1,) + a.shape[1:], lambda i, pt: (i,) + (0,) * (a.ndim - 1))
    exp = _expand_matrix(MOBA_BLOCK, t)
    return pl.pallas_call(
        functools.partial(_moba_decode_kernel, n_pages),
        grid_spec=pltpu.PrefetchScalarGridSpec(
            num_scalar_prefetch=1,
            grid=(b,),
            in_specs=page_specs + page_specs + [per_b(q_rows), per_b(new_rows),
                                                pl.BlockSpec(exp.shape, lambda i, pt: (0, 0))],
            out_specs=pl.BlockSpec((1, 1, hd), lambda i, pt: (i, 0, 0)),
        ),
        out_shape=jax.ShapeDtypeStruct((b, 1, hd), F32),
        compiler_params=_params("parallel"),
        name="moba_decode",
    )(page_table.reshape(-1), *([k_pool] * n_pages), *([v_pool] * n_pages), q_rows, new_rows, exp)


def _diff_decode_kernel(lambda_init, n_step, n_total, pt_ref, *refs):
    k_pages = refs[:n_step]
    v_pages = refs[n_step:2 * n_step]
    q_ref, knew_ref, vnew_ref, lam_ref, sg_ref, o_ref, m_ref, l_ref, acc_ref = refs[2 * n_step:]
    step = pl.program_id(1)
    d = HEAD_DIM
    nh = DIFF_HEADS
    t = n_total * PAGE_SIZE
    row = lax.broadcasted_iota(jnp.int32, (2 * nh, 1), 0)
    lane = lax.broadcasted_iota(jnp.int32, (1, 2 * d), 1)
    slope = jnp.exp2(-(row % nh + 1).astype(F32))
    q8 = q_ref[0] * QK_SCALE
    q = jnp.concatenate([jnp.where(lane < d, q8, 0.0), jnp.where(lane >= d, q8, 0.0)], axis=0)

    @pl.when(step == 0)
    def _():
        m_ref[...] = jnp.full(m_ref.shape, NEG_INF, F32)
        l_ref[...] = jnp.zeros_like(l_ref)
        acc_ref[...] = jnp.zeros_like(acc_ref)

    k_all = jnp.concatenate([pg[0].reshape(PAGE_SIZE * nh, 2 * d) for pg in k_pages], axis=0).astype(BF16)
    v_all = jnp.concatenate([pg[0].reshape(PAGE_SIZE * nh, 2 * d) for pg in v_pages], axis=0).astype(BF16)
    n_k = n_step * PAGE_SIZE * nh
    col = lax.broadcasted_iota(jnp.int32, (1, n_k), 1)
    pos = step * (n_step * PAGE_SIZE) + col // nh
    mask = col % nh == row % nh
    s = _nt_dot(q.astype(BF16), k_all) - slope * (t - pos).astype(F32)
    s = jnp.where(mask, s, NEG_INF)
    m_old = m_ref[:, 0:1]
    m_new = jnp.maximum(m_old, jnp.max(s, axis=-1, keepdims=True))
    alpha = jnp.exp(m_old - m_new)
    p = jnp.where(mask, jnp.exp(s - m_new), 0.0)
    l_new = alpha * l_ref[:, 0:1] + jnp.sum(p, axis=-1, keepdims=True)
    acc_new = alpha * acc_ref[...] + _dot(p.astype(BF16), v_all)
    m_ref[...] = jnp.broadcast_to(m_new, m_ref.shape)
    l_ref[...] = jnp.broadcast_to(l_new, l_ref.shape)
    acc_ref[...] = acc_new

    @pl.when(step == pl.num_programs(1) - 1)
    def _():
        k_new = jnp.concatenate([knew_ref[0]] * 2, axis=0)
        v_new = jnp.concatenate([vnew_ref[0]] * 2, axis=0)
        s_new = jnp.sum(q * k_new, axis=-1, keepdims=True)
        m_fin = jnp.maximum(m_new, s_new)
        a = jnp.exp(m_new - m_fin)
        p_new = jnp.exp(s_new - m_fin)
        o = (a * acc_new + p_new * v_new) / (a * l_new + p_new)
        lam = _diff_lambda(lam_ref, lambda_init)
        attn = o[:nh] - lam * o[nh:]
        o_ref[0] = _rms(attn, sg_ref[...]) * (1.0 - lambda_init)


def _diff_decode(page_table, k_pool, v_pool, q_rows, k_new, v_new, lam_rows, subln_gain, lambda_init):
    b, n_pages = page_table.shape
    nh, d2 = DIFF_HEADS, 2 * HEAD_DIM
    n_step = n_pages // 2
    page_specs = [pl.BlockSpec((1, PAGE_SIZE, nh, d2),
                               lambda i, s, pt, p=p: (pt[i * n_pages + s * n_step + p], 0, 0, 0))
                  for p in range(n_step)]
    per_b = pl.BlockSpec((1, nh, d2), lambda i, s, pt: (i, 0, 0))
    const = lambda a: pl.BlockSpec(a.shape, lambda i, s, pt: (0,) * a.ndim)
    sg = subln_gain.reshape(1, d2)
    return pl.pallas_call(
        functools.partial(_diff_decode_kernel, lambda_init, n_step, n_pages),
        grid_spec=pltpu.PrefetchScalarGridSpec(
            num_scalar_prefetch=1,
            grid=(b, n_pages // n_step),
            in_specs=page_specs + page_specs + [per_b, per_b, per_b, const(lam_rows), const(sg)],
            out_specs=pl.BlockSpec((1, nh, d2), lambda i, s, pt: (i, 0, 0)),
            scratch_shapes=[pltpu.VMEM((2 * nh, LANES), F32), pltpu.VMEM((2 * nh, LANES), F32),
                            pltpu.VMEM((2 * nh, d2), F32)],
        ),
        out_shape=jax.ShapeDtypeStruct((b, nh, d2), F32),
        compiler_params=_params("parallel", "arbitrary"),
        name="diff_decode",
    )(page_table.reshape(-1), *([k_pool] * n_step), *([v_pool] * n_step), q_rows, k_new, v_new, lam_rows, sg)


def _even_weights(w_in, w_out, q_gain, ks_gain, kw_gain, mq_gain, mk_gain):
    d, g_n, ha, hb = HEAD_DIM, NSA_KV_HEADS, NSA_HEADS, MOBA_HEADS
    sizes = [ha * d] + [g_n * d] * 6 + [3 * ha, hb * d, hb * d, hb * d]
    offs = np.concatenate([[0], np.cumsum(sizes)])
    q_n, kc, vc, ks, vs, kw, vw, gates, q_m, k_m, v_m = [w_in[:, offs[i]:offs[i + 1]] for i in range(11)]
    gates = gates.reshape(-1, ha, 3).transpose(0, 2, 1).reshape(-1, 3 * ha)
    pad = jnp.zeros((w_in.shape[0], E_COLS - E_GT - 3 * ha), w_in.dtype)
    w = jnp.concatenate([q_n, ks, kw, q_m, k_m, kc, vc, vs, vw, v_m, gates, pad], axis=1)
    ones = jnp.ones((E_COLS - E_KC,), F32)
    col_gain = jnp.concatenate([jnp.tile(q_gain, ha), jnp.tile(ks_gain, g_n), jnp.tile(kw_gain, g_n),
                                jnp.tile(mq_gain, hb), jnp.tile(mk_gain, hb), ones])
    w_out = w_out.astype(BF16)
    return w, col_gain, w_out[:ha * d], w_out[ha * d:]


def _odd_weights(w_in, w_out, q_gain, k_gain):
    n_half = 2 * DIFF_HEADS
    col_gain = jnp.concatenate([jnp.tile(q_gain, n_half), jnp.tile(k_gain, n_half),
                                jnp.ones((O_COLS - O_V,), F32)])
    return w_in.astype(BF16), col_gain, w_out.astype(BF16)


def _trunk_prompt(x, prm):
    b, t, dm = x.shape
    d, g_n = HEAD_DIM, NSA_KV_HEADS
    gd = g_n * d
    n_pages = t // PAGE_SIZE
    x = x.reshape(b * t, dm)
    x = _ffn(x, prm["ffn1_norm"][0], prm["ffn1_w_in"][0], prm["ffn1_w_out"][0])
    pe = _proj(x, prm["mix_norm"][0], prm["even_w"], prm["even_gain"], E_FLAGS)
    col = lambda off, width: pe[:, off:off + width]
    kc_rows, vc_rows = col(E_KC, gd), col(E_VC, gd)
    kc, vc = _compress_prompt(pe, prm["cmp_k"], prm["cmp_v"], prm["nsa_kc_gain"], b, t)
    o_nsa = _nsa_prompt(pe, kc, vc, b, t)
    o_moba = _moba_prompt(pe, b, t)
    x = _oproj(x, [o_nsa, o_moba], [prm["even_wo_nsa"], prm["even_wo_moba"]])
    x = _ffn(x, prm["ffn2_norm"][0], prm["ffn2_w_in"][0], prm["ffn2_w_out"][0])
    n_keep = min(WINDOW, t)
    last = lambda a: a.reshape(b, t, g_n, d)[:, t - n_keep:]
    even = (kc_rows.reshape(b, t, g_n, d), vc_rows.reshape(b, t, g_n, d),
            col(E_KS, gd).reshape(b, t, g_n, d), col(E_VS, gd).reshape(b, t, g_n, d),
            col(E_KM, MOBA_HEADS * d).reshape(b, t, MOBA_HEADS, d),
            col(E_VM, MOBA_HEADS * d).reshape(b, t, MOBA_HEADS, d),
            last(col(E_KW, gd)), last(col(E_VW, gd)))
    x = _ffn(x, prm["ffn1_norm"][1], prm["ffn1_w_in"][1], prm["ffn1_w_out"][1])
    po = _proj(x, prm["mix_norm"][1], prm["odd_w"], prm["odd_gain"], O_FLAGS)
    o_diff = _diff_prompt(po, prm["lam_rows"], prm["subln_gain"], prm["lambda_init"], b, t)
    x = _oproj(x, [o_diff], [prm["odd_wo"]])
    x = _ffn(x, prm["ffn2_norm"][1], prm["ffn2_w_in"][1], prm["ffn2_w_out"][1])
    odd = (po[:, O_K:O_V].reshape(b, t, DIFF_HEADS, 2 * d), po[:, O_V:].reshape(b, t, DIFF_HEADS, 2 * d))
    return x.reshape(b, t, dm), even, odd


def _trunk_sample(x, prm, past):
    b, t, dm = x.shape
    d, g_n = HEAD_DIM, NSA_KV_HEADS
    gd = g_n * d
    hm = MOBA_HEADS * d
    pt = past["page_table"]
    x = x.reshape(b, dm)
    x = _ffn(x, prm["ffn1_norm"][0], prm["ffn1_w_in32"][0], prm["ffn1_w_out32"][0])
    pe = _proj(x, prm["mix_norm"][0], prm["even_w32"], prm["even_gain"], E_FLAGS)
    col = lambda off, width: pe[:, off:off + width]
    pool_t = lambda c: jnp.transpose(c, (0, 2, 3, 1)).reshape(c.shape[0], -1, c.shape[1])
    kc, vc = _compress_decode(pt, pool_t(past["cmp_k"]), pool_t(past["cmp_v"]), prm["cmp_k"], prm["cmp_v"],
                              prm["nsa_kc_gain"])
    q8 = col(E_QN, NSA_HEADS * d).reshape(b, NSA_HEADS, d)
    gates = col(E_GT, 3 * NSA_HEADS).reshape(b, 3, NSA_HEADS).transpose(0, 2, 1)
    gates8 = jnp.pad(gates, ((0, 0), (0, 0), (0, LANES - 3)))
    new_nsa = jnp.stack([col(E_KS, gd), col(E_VS, gd), col(E_KW, gd), col(E_VW, gd)], axis=1)
    new_nsa = jnp.pad(new_nsa, ((0, 0), (0, 4), (0, 0)))
    o_nsa = _nsa_decode(pt, pool_t(past["slc_k"]), pool_t(past["slc_v"]), q8, gates8, kc, vc,
                        pool_t(past["win_k"]), pool_t(past["win_v"]), new_nsa).reshape(b, NSA_HEADS * d)
    new_moba = jnp.pad(jnp.stack([col(E_KM, hm), col(E_VM, hm)], axis=1), ((0, 0), (0, 6), (0, 0)))
    o_moba = _moba_decode(pt, pool_t(past["moba_k"]), pool_t(past["moba_v"]),
                          col(E_QM, hm).reshape(b, 1, hm), new_moba).reshape(b, hm)
    x = _oproj(x, [o_nsa, o_moba], [prm["even_wo_nsa"], prm["even_wo_moba"]])
    x = _ffn(x, prm["ffn2_norm"][0], prm["ffn2_w_in"][0], prm["ffn2_w_out"][0])
    roll = lambda win, new: jnp.concatenate([win[:, 1:], new.reshape(b, 1, g_n, d)], axis=1)
    r4 = lambda a, h: a.reshape(b, 1, h, a.shape[-1] // h)
    even = (r4(col(E_KC, gd), g_n), r4(col(E_VC, gd), g_n), r4(col(E_KS, gd), g_n), r4(col(E_VS, gd), g_n),
            r4(col(E_KM, hm), MOBA_HEADS), r4(col(E_VM, hm), MOBA_HEADS),
            roll(past["win_k"], col(E_KW, gd)), roll(past["win_v"], col(E_VW, gd)))
    x = _ffn(x, prm["ffn1_norm"][1], prm["ffn1_w_in"][1], prm["ffn1_w_out"][1])
    po = _proj(x, prm["mix_norm"][1], prm["odd_w"], prm["odd_gain"], O_FLAGS)
    heads = lambda a: a.reshape(b, DIFF_HEADS, 2 * d)
    o_diff = _diff_decode(pt, past["diff_k"], past["diff_v"], heads(po[:, :O_K]), heads(po[:, O_K:O_V]),
                          heads(po[:, O_V:]), prm["lam_rows"], prm["subln_gain"],
                          prm["lambda_init"]).reshape(b, DIFF_HEADS * 2 * d)
    x = _oproj(x, [o_diff], [prm["odd_wo"]])
    x = _ffn(x, prm["ffn2_norm"][1], prm["ffn2_w_in"][1], prm["ffn2_w_out"][1])
    odd = (r4(po[:, O_K:O_V], DIFF_HEADS), r4(po[:, O_V:], DIFF_HEADS))
    return x.reshape(b, t, dm), even, odd


def kernel(x_prompt, x_sample, cache_nsa_cmp_k, cache_nsa_cmp_v, cache_nsa_slc_k, cache_nsa_slc_v,
           cache_moba_k, cache_moba_v, state_nsa_win_k, state_nsa_win_v, cache_diff_k, cache_diff_v,
           page_table, ffn1_norm, ffn1_w_in, ffn1_w_out, mix_norm, ffn2_norm, ffn2_w_in, ffn2_w_out,
           even_w_in, even_w_out, nsa_q_gain, nsa_kc_gain, nsa_ks_gain, nsa_kw_gain,
           cmp_k_pe, cmp_k_w1, cmp_k_w2, cmp_v_pe, cmp_v_w1, cmp_v_w2, moba_q_gain, moba_k_gain,
           odd_w_in, odd_w_out, diff_q_gain, diff_k_gain, diff_lq1, diff_lk1, diff_lq2, diff_lk2,
           diff_subln_gain):
    even_w, even_gain, wo_nsa, wo_moba = _even_weights(even_w_in[0], even_w_out[0], nsa_q_gain[0], nsa_ks_gain[0],
                                                       nsa_kw_gain[0], moba_q_gain[0], moba_k_gain[0])
    odd_w, odd_gain, odd_wo = _odd_weights(odd_w_in[0], odd_w_out[0], diff_q_gain[0], diff_k_gain[0])
    lam_rows = jnp.concatenate([diff_lq1, diff_lk1, diff_lq2, diff_lk2, jnp.zeros((4, HEAD_DIM), F32)], axis=0)
    prm = {
        "ffn1_norm": ffn1_norm, "ffn1_w_in": ffn1_w_in.astype(BF16), "ffn1_w_out": ffn1_w_out.astype(BF16),
        "ffn2_norm": ffn2_norm, "ffn2_w_in": ffn2_w_in.astype(BF16), "ffn2_w_out": ffn2_w_out.astype(BF16),
        "mix_norm": mix_norm,
        "even_w": even_w.astype(BF16), "even_gain": even_gain, "even_wo_nsa": wo_nsa, "even_wo_moba": wo_moba,
        "even_w32": even_w, "ffn1_w_in32": ffn1_w_in, "ffn1_w_out32": ffn1_w_out,
        "cmp_k": _cmp_weights(cmp_k_pe[0], cmp_k_w1[0], cmp_k_w2[0]),
        "cmp_v": _cmp_weights(cmp_v_pe[0], cmp_v_w1[0], cmp_v_w2[0]),
        "nsa_kc_gain": nsa_kc_gain[0],
        "odd_w": odd_w, "odd_gain": odd_gain, "odd_wo": odd_wo,
        "lam_rows": lam_rows, "subln_gain": diff_subln_gain[0],
        "lambda_init": 0.8 - 0.6 * math.exp(-0.3 * 1),
    }
    past = {
        "page_table": page_table,
        "cmp_k": cache_nsa_cmp_k[0], "cmp_v": cache_nsa_cmp_v[0],
        "slc_k": cache_nsa_slc_k[0], "slc_v": cache_nsa_slc_v[0],
        "moba_k": cache_moba_k[0], "moba_v": cache_moba_v[0],
        "win_k": state_nsa_win_k[0], "win_v": state_nsa_win_v[0],
        "diff_k": cache_diff_k[0], "diff_v": cache_diff_v[0],
    }
    y_p, even_p, odd_p = _trunk_prompt(x_prompt, prm)
    y_s, even_s, odd_s = _trunk_sample(x_sample, prm, past)
    outs = [y_p, y_s]
    for p_leaf, s_leaf in zip(even_p + odd_p, even_s + odd_s):
        outs += [p_leaf[None], s_leaf[None]]
    return tuple(outs)
```

```python
import functools
import math

import numpy as np
import jax
import jax.numpy as jnp
from jax import lax
from jax.experimental import pallas as pl
from jax.experimental.pallas import tpu as pltpu

F32 = jnp.float32
BF16 = jnp.bfloat16
HIGHEST = lax.Precision.HIGHEST

D_MODEL = 1024
HEAD_DIM = 64
NSA_HEADS = 8
NSA_KV_HEADS = 2
NSA_GROUP = NSA_HEADS // NSA_KV_HEADS
CMP_BLOCK = 32
CMP_STRIDE = 16
CMP_HIDDEN = 2 * HEAD_DIM
SLC_BLOCK = 64
SLC_TOP = 3
WINDOW = 256
MOBA_HEADS = 8
MOBA_BLOCK = 256
MOBA_TOP = 3
DIFF_HEADS = D_MODEL // (2 * HEAD_DIM)
PAGE_SIZE = 128
RMS_EPS = 1e-6
NEG_INF = -1e30
QK_SCALE = HEAD_DIM ** -0.5
LANES = 128
MXU_N = 256
VMEM_LIMIT = 56 * 1024 * 1024

E_QN, E_KS, E_KW, E_QM, E_KM = 0, 512, 640, 768, 1280
E_KC, E_VC, E_VS, E_VW, E_VM, E_GT = 1792, 1920, 2048, 2176, 2304, 2816
E_COLS = 3072
E_FLAGS = (1, 1, 1, 1, 1, 1, 1, 0, 0, 0, 0, 0)
O_Q, O_K, O_V = 0, 1024, 2048
O_COLS = 3072
O_FLAGS = (1, 1, 1, 1, 1, 1, 1, 1, 0, 0, 0, 0)


def _nt_dot(a, b, precision=None):
    return lax.dot_general(a, b, (((1,), (1,)), ((), ())), precision=precision,
                           preferred_element_type=F32)


def _dot(a, b, precision=None):
    return jnp.dot(a, b, precision=precision, preferred_element_type=F32)


def _mm(a, w):
    if w.dtype == F32:
        return _dot(a.astype(F32), w, precision=HIGHEST)
    return _dot(a.astype(w.dtype), w)


def _rms(x, gain):
    return x * lax.rsqrt(jnp.mean(x * x, axis=-1, keepdims=True) + RMS_EPS) * gain


def _params(*sem):
    return pltpu.CompilerParams(dimension_semantics=sem, vmem_limit_bytes=VMEM_LIMIT)


def _ffn_kernel(x_ref, g_ref, wg_ref, wu_ref, wo_ref, o_ref, xn_ref, acc_ref):
    j = pl.program_id(1)

    @pl.when(j == 0)
    def _():
        xn_ref[...] = _rms(x_ref[...], g_ref[...]).astype(xn_ref.dtype)
        acc_ref[...] = jnp.zeros_like(acc_ref)

    xn = xn_ref[...]
    gate = _mm(xn, wg_ref[...])
    up = _mm(xn, wu_ref[...])
    acc_ref[...] += _mm(gate * jax.nn.sigmoid(gate) * up, wo_ref[...])

    @pl.when(j == pl.num_programs(1) - 1)
    def _():
        o_ref[...] = x_ref[...] + 0.5 * acc_ref[...]


def _ffn(x, gain, w_in, w_out):
    m, d = x.shape
    f = w_out.shape[0]
    tm = min(m, 512)
    tf = f // 2
    nf = f // tf
    return pl.pallas_call(
        _ffn_kernel,
        grid=(m // tm, nf),
        in_specs=[
            pl.BlockSpec((tm, d), lambda i, j: (i, 0)),
            pl.BlockSpec((1, d), lambda i, j: (0, 0)),
            pl.BlockSpec((d, tf), lambda i, j: (0, j)),
            pl.BlockSpec((d, tf), lambda i, j: (0, j + nf)),
            pl.BlockSpec((tf, d), lambda i, j: (j, 0)),
        ],
        out_specs=pl.BlockSpec((tm, d), lambda i, j: (i, 0)),
        out_shape=jax.ShapeDtypeStruct((m, d), F32),
        scratch_shapes=[pltpu.VMEM((tm, d), w_in.dtype), pltpu.VMEM((tm, d), F32)],
        compiler_params=_params("parallel", "arbitrary"),
        name="ffn",
    )(x, gain.reshape(1, d), w_in, w_in, w_out)


def _proj_kernel(flag_ref, x_ref, g_ref, w_ref, cg_ref, bd_ref, o_ref, xn_ref):
    j = pl.program_id(1)

    @pl.when(j == 0)
    def _():
        xn_ref[...] = _rms(x_ref[...], g_ref[...]).astype(xn_ref.dtype)

    y = _mm(xn_ref[...], w_ref[...])

    @pl.when(flag_ref[j] == 1)
    def _():
        y2 = y * y
        hi = y2.astype(BF16)
        lo = (y2 - hi.astype(F32)).astype(BF16)
        ss = _dot(hi, bd_ref[...]) + _dot(lo, bd_ref[...])
        o_ref[...] = y * lax.rsqrt(ss * (1.0 / HEAD_DIM) + RMS_EPS) * cg_ref[...]

    @pl.when(flag_ref[j] == 0)
    def _():
        o_ref[...] = y


def _proj(x, gain, w, col_gain, flags):
    m, d = x.shape
    n = w.shape[1]
    tm = min(m, 1024)
    tn = MXU_N
    head = np.arange(tn) // HEAD_DIM
    bd = jnp.asarray(head[:, None] == head[None, :], BF16)
    return pl.pallas_call(
        _proj_kernel,
        grid_spec=pltpu.PrefetchScalarGridSpec(
            num_scalar_prefetch=1,
            grid=(m // tm, n // tn),
            in_specs=[
                pl.BlockSpec((tm, d), lambda i, j, fl: (i, 0)),
                pl.BlockSpec((1, d), lambda i, j, fl: (0, 0)),
                pl.BlockSpec((d, tn), lambda i, j, fl: (0, j)),
                pl.BlockSpec((1, tn), lambda i, j, fl: (0, j)),
                pl.BlockSpec((tn, tn), lambda i, j, fl: (0, 0)),
            ],
            out_specs=pl.BlockSpec((tm, tn), lambda i, j, fl: (i, j)),
            scratch_shapes=[pltpu.VMEM((tm, d), w.dtype)],
        ),
        out_shape=jax.ShapeDtypeStruct((m, n), F32),
        compiler_params=_params("parallel", "arbitrary"),
        name="proj",
    )(jnp.asarray(flags, jnp.int32), x, gain.reshape(1, d), w, col_gain.reshape(1, n), bd)


def _oproj_kernel(n_in, x_ref, *refs):
    o_ref = refs[2 * n_in]
    acc = x_ref[...]
    for a_ref, w_ref in zip(refs[:n_in], refs[n_in:2 * n_in]):
        acc = acc + _dot(a_ref[...].astype(BF16), w_ref[...])
    o_ref[...] = acc


def _oproj(x, acts, ws):
    m, d = x.shape
    tm = min(m, 512)
    n_in = len(acts)
    in_specs = [pl.BlockSpec((tm, d), lambda i: (i, 0))]
    in_specs += [pl.BlockSpec((tm, a.shape[1]), lambda i: (i, 0)) for a in acts]
    in_specs += [pl.BlockSpec(w.shape, lambda i: (0, 0)) for w in ws]
    return pl.pallas_call(
        functools.partial(_oproj_kernel, n_in),
        grid=(m // tm,),
        in_specs=in_specs,
        out_specs=pl.BlockSpec((tm, d), lambda i: (i, 0)),
        out_shape=jax.ShapeDtypeStruct((m, d), F32),
        compiler_params=_params("parallel"),
        name="oproj",
    )(x, *acts, *ws)


def _top_select(cur, lane, own, n_top):
    sel = jnp.zeros(cur.shape, F32)
    for _ in range(n_top):
        mx = jnp.max(cur, axis=-1, keepdims=True)
        idx = jnp.min(jnp.where(cur == mx, lane, 4 * LANES), axis=-1, keepdims=True)
        pick = lane == idx
        sel = jnp.where(pick, jnp.where(lane < own, 1.0, 0.0), sel)
        cur = jnp.where(pick, -3e38, cur)
    return sel


def _softmax_update(carry, s, mask, v):
    m, l, acc = carry
    s = jnp.where(mask, s, NEG_INF)
    m_new = jnp.maximum(m, jnp.max(s, axis=-1, keepdims=True))
    alpha = jnp.exp(m - m_new)
    p = jnp.where(mask, jnp.exp(s - m_new), 0.0)
    l = alpha * l + jnp.sum(p, axis=-1, keepdims=True)
    acc = alpha * acc + _dot(p.astype(BF16), v)
    return m_new, l, acc


def _softmax_init(rows, width):
    return (jnp.full((rows, 1), NEG_INF, F32), jnp.zeros((rows, 1), F32), jnp.zeros((rows, width), F32))


def _safe_div(acc, l):
    return jnp.where(l > 0.0, acc / jnp.where(l > 0.0, l, 1.0), 0.0)


def _compress_rows(scr, n_c, w1_ref, w2_ref, pe_ref):
    x = jnp.concatenate([scr[pl.ds(r, n_c, stride=CMP_STRIDE), :] for r in range(CMP_BLOCK)], axis=1)
    hid = _dot(x.astype(BF16), w1_ref[...]) + _dot(pe_ref[...].astype(BF16), w1_ref[...])[0:1]
    hid = hid * jax.nn.sigmoid(hid)
    return _dot(hid.astype(BF16), w2_ref[...])


def _store_kc(kc, gk_ref, kc_ref):
    g = gk_ref[...]
    for grp in range(NSA_KV_HEADS):
        sl = slice(grp * HEAD_DIM, (grp + 1) * HEAD_DIM)
        kc_ref[0, :, sl] = _rms(kc[:, sl], g[:, sl])


def _cmp_prompt_kernel(k_ref, v_ref, w1k, w2k, pek, gk, w1v, w2v, pev, kc_ref, vc_ref, scr):
    t = k_ref.shape[0]
    n_c = t // CMP_STRIDE
    scr[t:, :] = jnp.zeros((scr.shape[0] - t, scr.shape[1]), F32)
    scr[:t, :] = k_ref[...]
    _store_kc(_compress_rows(scr, n_c, w1k, w2k, pek), gk, kc_ref)
    scr[:t, :] = v_ref[...]
    vc_ref[0] = _compress_rows(scr, n_c, w1v, w2v, pev)


def _cmp_decode_kernel(n_pages, pt_ref, *refs):
    k_pages = refs[:n_pages]
    v_pages = refs[n_pages:2 * n_pages]
    (w1k, w2k, pek, gk, w1v, w2v, pev, kc_ref, vc_ref, scr) = refs[2 * n_pages:]
    t = n_pages * PAGE_SIZE
    n_c = t // CMP_STRIDE
    scr[t:, :] = jnp.zeros((scr.shape[0] - t, scr.shape[1]), F32)
    for p, pg in enumerate(k_pages):
        scr[p * PAGE_SIZE:(p + 1) * PAGE_SIZE, :] = pg[0].T
    _store_kc(_compress_rows(scr, n_c, w1k, w2k, pek), gk, kc_ref)
    for p, pg in enumerate(v_pages):
        scr[p * PAGE_SIZE:(p + 1) * PAGE_SIZE, :] = pg[0].T
    vc_ref[0] = _compress_rows(scr, n_c, w1v, w2v, pev)


def _cmp_weights(pe, w1, w2):
    d, hdn, g_n = HEAD_DIM, CMP_HIDDEN, NSA_KV_HEADS
    w1r = w1.reshape(CMP_BLOCK, d, hdn)
    eye = jnp.eye(g_n, dtype=F32)
    w1b = jnp.einsum("rdk,gq->rgdqk", w1r, eye).reshape(CMP_BLOCK * g_n * d, g_n * hdn)
    w2b = jnp.einsum("kd,gq->gkqd", w2, eye).reshape(g_n * hdn, g_n * d)
    per = jnp.broadcast_to(pe.reshape(CMP_BLOCK, 1, d), (CMP_BLOCK, g_n, d)).reshape(1, -1)
    pe8 = jnp.concatenate([per, jnp.zeros((7, per.shape[1]), F32)], axis=0)
    return w1b.astype(BF16), w2b.astype(BF16), pe8


def _cmp_consts(wk, wv, kc_gain):
    gk = jnp.tile(kc_gain, NSA_KV_HEADS).reshape(1, NSA_KV_HEADS * HEAD_DIM)
    return [*wk, gk, *wv]


def _compress_prompt(p_even, wk, wv, kc_gain, b, t):
    n_c = t // CMP_STRIDE
    gd = NSA_KV_HEADS * HEAD_DIM
    consts = _cmp_consts(wk, wv, kc_gain)
    return pl.pallas_call(
        _cmp_prompt_kernel,
        grid=(b,),
        in_specs=[pl.BlockSpec((t, gd), lambda i: (i, E_KC // gd)), pl.BlockSpec((t, gd), lambda i: (i, E_VC // gd))]
        + [pl.BlockSpec(a.shape, lambda i: (0, 0)) for a in consts],
        out_specs=[pl.BlockSpec((1, n_c, gd), lambda i: (i, 0, 0))] * 2,
        out_shape=[jax.ShapeDtypeStruct((b, n_c, gd), F32)] * 2,
        scratch_shapes=[pltpu.VMEM((t + CMP_STRIDE, gd), F32)],
        compiler_params=_params("parallel"),
        name="nsa_compress_prompt",
    )(p_even, p_even, *consts)


def _compress_decode(page_table, k_pool_t, v_pool_t, wk, wv, kc_gain):
    b, n_pages = page_table.shape
    t = n_pages * PAGE_SIZE
    n_c = t // CMP_STRIDE
    gd = NSA_KV_HEADS * HEAD_DIM
    page_specs = [pl.BlockSpec((1, gd, PAGE_SIZE), lambda i, pt, p=p: (pt[i * n_pages + p], 0, 0))
                  for p in range(n_pages)]
    consts = _cmp_consts(wk, wv, kc_gain)
    return pl.pallas_call(
        functools.partial(_cmp_decode_kernel, n_pages),
        grid_spec=pltpu.PrefetchScalarGridSpec(
            num_scalar_prefetch=1,
            grid=(b,),
            in_specs=page_specs + page_specs + [pl.BlockSpec(a.shape, lambda i, pt: (0, 0)) for a in consts],
            out_specs=[pl.BlockSpec((1, n_c, gd), lambda i, pt: (i, 0, 0))] * 2,
            scratch_shapes=[pltpu.VMEM((t + CMP_STRIDE, gd), F32)],
        ),
        out_shape=[jax.ShapeDtypeStruct((b, n_c, gd), F32)] * 2,
        compiler_params=_params("parallel"),
        name="nsa_compress_decode",
    )(page_table.reshape(-1), *([k_pool_t] * n_pages), *([v_pool_t] * n_pages), *consts)


def _cover_matrix():
    j = np.arange(LANES)[:, None]
    n = np.arange(LANES)[None, :]
    c0 = j * CMP_STRIDE
    s0 = n * SLC_BLOCK
    return jnp.asarray((c0 < s0 + SLC_BLOCK) & (c0 + CMP_BLOCK - 1 >= s0), F32)


def _expand_matrix(block, t):
    n = np.arange(LANES)[:, None]
    s = np.arange(t)[None, :]
    return jnp.asarray(s // block == n, BF16)


def _nsa_kernel(q_ref, gt_ref, ksw_ref, vsw_ref, kc_ref, vc_ref, cov_ref, exp_ref, o_ref, selk_ref):
    i = pl.program_id(1)
    tq_n = q_ref.shape[0]
    r_n = NSA_GROUP
    d = HEAD_DIM
    t0 = i * tq_n
    rows = r_n * tq_n
    row = lax.broadcasted_iota(jnp.int32, (rows, 1), 0)
    tq = t0 + row % tq_n
    tq1 = tq[:tq_n]
    lane = lax.broadcasted_iota(jnp.int32, (1, LANES), 1)
    gates = jax.nn.sigmoid(gt_ref[:, :LANES])
    own = tq1 // SLC_BLOCK

    for g in range(NSA_KV_HEADS):
        kcol = slice(g * d, (g + 1) * d)
        wcol = slice(LANES + g * d, LANES + (g + 1) * d)
        qg = jnp.concatenate([q_ref[:, (g * r_n + r) * d:(g * r_n + r + 1) * d] for r in range(r_n)], axis=0)
        qg = (qg * QK_SCALE).astype(BF16)
        slope = jnp.exp2(-(g * r_n + row // tq_n + 1).astype(F32))

        c_end = lane * CMP_STRIDE + (CMP_BLOCK - 1)
        s = _nt_dot(qg, kc_ref[0, :, kcol].astype(BF16)) - slope * (tq - c_end).astype(F32)
        valid = c_end <= tq
        s = jnp.where(valid, s, NEG_INF)
        p_c = jnp.where(valid, jnp.exp(s - jnp.max(s, axis=-1, keepdims=True)), 0.0)
        p_c = _safe_div(p_c, jnp.sum(p_c, axis=-1, keepdims=True))
        o_c = _dot(p_c.astype(BF16), vc_ref[0, :, kcol].astype(BF16))

        p_sum = p_c[0:tq_n]
        for r in range(1, r_n):
            p_sum = p_sum + p_c[r * tq_n:(r + 1) * tq_n]
        imp = _dot(p_sum, cov_ref[...], precision=HIGHEST)
        cur = jnp.where(lane < own, imp, -1.0)
        sel = _top_select(cur, lane, own, SLC_TOP)
        sel = jnp.where(lane == own, 1.0, sel)
        selk_ref[...] = _dot(sel.astype(BF16), exp_ref[...])

        def slc_body(j, carry):
            k0 = pl.multiple_of(j * tq_n, tq_n)
            kt = ksw_ref[pl.ds(k0, tq_n), kcol].astype(BF16)
            vt = vsw_ref[pl.ds(k0, tq_n), kcol].astype(BF16)
            pos = k0 + lane
            s = _nt_dot(qg, kt) - slope * (tq - pos).astype(F32)
            mk = selk_ref[:, pl.ds(k0, tq_n)]
            mk = jnp.concatenate([mk] * r_n, axis=0) > 0.5
            return _softmax_update(carry, s, mk & (pos <= tq), vt)

        _, l_s, acc_s = lax.fori_loop(0, i + 1, slc_body, _softmax_init(rows, d))
        o_s = _safe_div(acc_s, l_s)

        carry = _softmax_init(rows, d)
        for jj in range(WINDOW // tq_n + 1):
            tile = i - WINDOW // tq_n + jj
            k0 = pl.multiple_of(jnp.maximum(tile, 0) * tq_n, tq_n)
            kt = ksw_ref[pl.ds(k0, tq_n), wcol].astype(BF16)
            vt = vsw_ref[pl.ds(k0, tq_n), wcol].astype(BF16)
            pos = tile * tq_n + lane
            dist = tq - pos
            s = _nt_dot(qg, kt) - slope * dist.astype(F32)
            carry = _softmax_update(carry, s, (pos >= 0) & (dist >= 0) & (dist <= WINDOW), vt)
        o_w = _safe_div(carry[2], carry[1])

        for r in range(r_n):
            h = g * r_n + r
            rs = slice(r * tq_n, (r + 1) * tq_n)
            o_ref[:, h * d:(h + 1) * d] = (gates[:, h:h + 1] * o_c[rs]
                                           + gates[:, NSA_HEADS + h:NSA_HEADS + h + 1] * o_s[rs]
                                           + gates[:, 2 * NSA_HEADS + h:2 * NSA_HEADS + h + 1] * o_w[rs])


def _nsa_prompt(p_even, kc, vc, b, t):
    tq_n = LANES
    nq = t // tq_n
    cov = _cover_matrix()
    exp = _expand_matrix(SLC_BLOCK, t)
    return pl.pallas_call(
        _nsa_kernel,
        grid=(b, nq),
        in_specs=[
            pl.BlockSpec((tq_n, NSA_HEADS * HEAD_DIM), lambda bi, i: (bi * nq + i, E_QN // 512)),
            pl.BlockSpec((tq_n, 2 * LANES), lambda bi, i: (bi * nq + i, E_GT // 256)),
            pl.BlockSpec((t, 2 * LANES), lambda bi, i: (bi, E_KS // 256)),
            pl.BlockSpec((t, 2 * LANES), lambda bi, i: (bi, E_VS // 256)),
            pl.BlockSpec((1, LANES, LANES), lambda bi, i: (bi, 0, 0)),
            pl.BlockSpec((1, LANES, LANES), lambda bi, i: (bi, 0, 0)),
            pl.BlockSpec((LANES, LANES), lambda bi, i: (0, 0)),
            pl.BlockSpec((LANES, t), lambda bi, i: (0, 0)),
        ],
        out_specs=pl.BlockSpec((tq_n, NSA_HEADS * HEAD_DIM), lambda bi, i: (bi * nq + i, 0)),
        out_shape=jax.ShapeDtypeStruct((b * t, NSA_HEADS * HEAD_DIM), F32),
        scratch_shapes=[pltpu.VMEM((tq_n, t), F32)],
        compiler_params=_params("parallel", "arbitrary"),
        name="nsa_prompt",
    )(p_even, p_even, p_even, p_even, kc, vc, cov, exp)


def _moba_kernel(q_ref, k_ref, v_ref, o_ref, km_ref):
    hp = pl.program_id(1)
    i = pl.program_id(2)
    tq_n = q_ref.shape[0]
    t = k_ref.shape[0]
    n_b = t // MOBA_BLOCK
    d = HEAD_DIM

    @pl.when(i == 0)
    def _():
        km_ref[...] = jnp.zeros_like(km_ref)
        for n in range(n_b):
            km_ref[n:n + 1, :] = jnp.mean(k_ref[n * MOBA_BLOCK:(n + 1) * MOBA_BLOCK, :], axis=0, keepdims=True)

    t0 = i * tq_n
    tq = t0 + lax.broadcasted_iota(jnp.int32, (tq_n, 1), 0)
    lane = lax.broadcasted_iota(jnp.int32, (1, LANES), 1)
    kpos = lax.broadcasted_iota(jnp.int32, (1, MOBA_BLOCK), 1)
    own = i

    for hh in range(2):
        col = slice(hh * d, (hh + 1) * d)
        hvec = jnp.zeros((1, 1), jnp.int32) + (hp * 2 + hh + 1)
        slope = jnp.exp2(-hvec.astype(F32))
        q = q_ref[:, col]
        route = _nt_dot(q, km_ref[:, col], precision=HIGHEST)
        cur = jnp.where(lane < own, route, NEG_INF)
        cur = jnp.where(lane < n_b, cur, -2e38)
        sel = _top_select(cur, lane, own, MOBA_TOP)
        sel = jnp.where(lane == own, 1.0, sel)
        qb = (q * QK_SCALE).astype(BF16)

        def body(n, carry):
            k0 = pl.multiple_of(n * MOBA_BLOCK, MOBA_BLOCK)
            kt = k_ref[pl.ds(k0, MOBA_BLOCK), col].astype(BF16)
            vt = v_ref[pl.ds(k0, MOBA_BLOCK), col].astype(BF16)
            pos = k0 + kpos
            s = _nt_dot(qb, kt) - slope * (tq - pos).astype(F32)
            chosen = jnp.sum(jnp.where(lane == n, sel, 0.0), axis=-1, keepdims=True) > 0.5
            return _softmax_update(carry, s, chosen & (pos <= tq), vt)

        _, l, acc = lax.fori_loop(0, i + 1, body, _softmax_init(tq_n, d))
        o_ref[:, col] = _safe_div(acc, l)


def _moba_prompt(p_even, b, t):
    tq_n = MOBA_BLOCK
    nq = t // tq_n
    return pl.pallas_call(
        _moba_kernel,
        grid=(b, MOBA_HEADS // 2, nq),
        in_specs=[
            pl.BlockSpec((tq_n, LANES), lambda bi, hp, i: (bi * nq + i, E_QM // LANES + hp)),
            pl.BlockSpec((t, LANES), lambda bi, hp, i: (bi, E_KM // LANES + hp)),
            pl.BlockSpec((t, LANES), lambda bi, hp, i: (bi, E_VM // LANES + hp)),
        ],
        out_specs=pl.BlockSpec((tq_n, LANES), lambda bi, hp, i: (bi * nq + i, hp)),
        out_shape=jax.ShapeDtypeStruct((b * t, MOBA_HEADS * HEAD_DIM), F32),
        scratch_shapes=[pltpu.VMEM((LANES, LANES), F32)],
        compiler_params=_params("parallel", "parallel", "arbitrary"),
        name="moba_prompt",
    )(p_even, p_even, p_even)


def _diff_lambda(lam_ref, lambda_init):
    lv = lam_ref[...]
    a = jnp.sum(lv[0:1] * lv[1:2], axis=-1, keepdims=True)
    c = jnp.sum(lv[2:3] * lv[3:4], axis=-1, keepdims=True)
    return jnp.exp(a) - jnp.exp(c) + lambda_init


def _diff_kernel(lambda_init, q_ref, k_ref, v_ref, lam_ref, sg_ref, o_ref):
    h = pl.program_id(1)
    i = pl.program_id(2)
    tq_n = q_ref.shape[0]
    d2 = 2 * HEAD_DIM
    t0 = i * tq_n
    row = lax.broadcasted_iota(jnp.int32, (2 * tq_n, 1), 0)
    tq = t0 + row % tq_n
    lane = lax.broadcasted_iota(jnp.int32, (1, d2), 1)
    kpos = lax.broadcasted_iota(jnp.int32, (1, tq_n), 1)
    slope = jnp.exp2(-(jnp.zeros((1, 1), jnp.int32) + (h + 1)).astype(F32))
    q = q_ref[...] * QK_SCALE
    q2 = jnp.concatenate([jnp.where(lane < HEAD_DIM, q, 0.0), jnp.where(lane >= HEAD_DIM, q, 0.0)], axis=0)
    q2 = q2.astype(BF16)

    def body(j, carry):
        k0 = pl.multiple_of(j * tq_n, tq_n)
        kt = k_ref[pl.ds(k0, tq_n), :].astype(BF16)
        vt = v_ref[pl.ds(k0, tq_n), :].astype(BF16)
        pos = k0 + kpos
        s = _nt_dot(q2, kt) - slope * (tq - pos).astype(F32)
        return _softmax_update(carry, s, pos <= tq, vt)

    _, l, acc = lax.fori_loop(0, i + 1, body, _softmax_init(2 * tq_n, d2))
    o = acc / l
    lam = _diff_lambda(lam_ref, lambda_init)
    attn = o[:tq_n] - lam * o[tq_n:]
    o_ref[...] = _rms(attn, sg_ref[...]) * (1.0 - lambda_init)


def _diff_prompt(p_odd, lam_rows, subln_gain, lambda_init, b, t):
    tq_n = 256
    nq = t // tq_n
    d2 = 2 * HEAD_DIM
    return pl.pallas_call(
        functools.partial(_diff_kernel, lambda_init),
        grid=(b, DIFF_HEADS, nq),
        in_specs=[
            pl.BlockSpec((tq_n, d2), lambda bi, h, i: (bi * nq + i, O_Q // d2 + h)),
            pl.BlockSpec((t, d2), lambda bi, h, i: (bi, O_K // d2 + h)),
            pl.BlockSpec((t, d2), lambda bi, h, i: (bi, O_V // d2 + h)),
            pl.BlockSpec(lam_rows.shape, lambda bi, h, i: (0, 0)),
            pl.BlockSpec((1, d2), lambda bi, h, i: (0, 0)),
        ],
        out_specs=pl.BlockSpec((tq_n, d2), lambda bi, h, i: (bi * nq + i, h)),
        out_shape=jax.ShapeDtypeStruct((b * t, DIFF_HEADS * d2), F32),
        compiler_params=_params("parallel", "parallel", "arbitrary"),
        name="diff_prompt",
    )(p_odd, p_odd, p_odd, lam_rows, subln_gain.reshape(1, d2))


def _tflash_init(dv, n):
    return (jnp.full((1, n), NEG_INF, F32), jnp.zeros((1, n), F32), jnp.zeros((dv, n), F32))


def _tflash_step(state, kt, qt, bias, vts, mask=None):
    m, l, acc = state
    st = _dot(kt, qt) + bias
    if mask is not None:
        st = jnp.where(mask, st, NEG_INF)
    m_new = jnp.maximum(m, jnp.max(st, axis=0, keepdims=True))
    alpha = jnp.exp(m - m_new)
    p = jnp.exp(st - m_new)
    l = alpha * l + jnp.sum(p, axis=0, keepdims=True)
    pb = p.astype(BF16)
    seg = pb.shape[1] // len(vts)
    pv = [_dot(vt, pb[:, s * seg:(s + 1) * seg]) for s, vt in enumerate(vts)]
    acc = alpha * acc + (pv[0] if len(pv) == 1 else jnp.concatenate(pv, axis=1))
    return m_new, l, acc


def _top_select_rows(cur, row, own, n_top):
    sel = jnp.zeros(cur.shape, F32)
    for _ in range(n_top):
        mx = jnp.max(cur, axis=0, keepdims=True)
        idx = jnp.min(jnp.where(cur == mx, row, 4 * LANES), axis=0, keepdims=True)
        pick = row == idx
        sel = jnp.where(pick, jnp.where(row < own, 1.0, 0.0), sel)
        cur = jnp.where(pick, -3e38, cur)
    return sel


def _transpose_into(dst_ref, src_ref, col0):
    for n in range(src_ref.shape[0] // LANES):
        dst_ref[:, n * LANES:(n + 1) * LANES] = src_ref[n * LANES:(n + 1) * LANES, col0:col0 + LANES].T


def _lane_tile(x, n):
    return x if n == 1 else jnp.concatenate([x] * n, axis=1)


def _pos_lanes(t):
    return jnp.asarray(np.broadcast_to(np.arange(t, dtype=np.float32)[:, None], (t, LANES)))


def _nsa_attn_kernel(q_ref, gt_ref, ksw_ref, vsw_ref, kc_ref, vc_ref, covt_ref, expt_ref, pos_ref, o_ref,
                     vst_ref, vwt_ref, vct_ref, selk_ref):
    i = pl.program_id(1)
    nq = q_ref.shape[0]
    d, r_n = HEAD_DIM, NSA_GROUP
    n_cb = NSA_HEADS * d // LANES

    @pl.when(i == 0)
    def _():
        _transpose_into(vst_ref, vsw_ref, 0)
        _transpose_into(vwt_ref, vsw_ref, LANES)
        vct_ref[...] = vc_ref[0].T

    t0 = i * nq
    lane = lax.broadcasted_iota(jnp.int32, (1, nq), 1)
    row = lax.broadcasted_iota(jnp.int32, (LANES, 1), 0)
    tq = t0 + lane
    own = tq // SLC_BLOCK
    tk = r_n * nq
    krow = lax.broadcasted_iota(jnp.int32, (tk, 1), 0)
    gates_t = jax.nn.sigmoid(gt_ref[:, :LANES]).T
    q_t = [q_ref[:, cb * LANES:(cb + 1) * LANES].T for cb in range(n_cb)]
    c_end = pos_ref[0:LANES, :] * CMP_STRIDE + (CMP_BLOCK - 1)
    c_valid = c_end <= tq.astype(F32)
    head_out = [None] * NSA_HEADS

    for g in range(NSA_KV_HEADS):
        g_rows = row // d == g
        heads = [g * r_n + r for r in range(r_n)]
        slopes = [2.0 ** -(h + 1) for h in heads]

        def head_qt(h):
            x = q_t[h // 2]
            if h % 2 != g:
                x = pltpu.roll(x, d, 0)
            return (jnp.where(g_rows, x, 0.0) * QK_SCALE).astype(BF16)

        qt = jnp.concatenate([head_qt(h) for h in heads], axis=1)
        n_all = r_n * nq

        def head_bias(base):
            return jnp.concatenate([s * base for s in slopes], axis=1)

        vrows = slice(g * d, (g + 1) * d)

        cm = _lane_tile(c_valid, r_n)
        s = jnp.where(cm, _dot(kc_ref[0].astype(BF16), qt) + head_bias(c_end), NEG_INF)
        p = jnp.where(cm, jnp.exp(s - jnp.max(s, axis=0, keepdims=True)), 0.0)
        l = jnp.sum(p, axis=0, keepdims=True)
        p = p * jnp.where(l > 0.0, 1.0 / jnp.where(l > 0.0, l, 1.0), 0.0)
        o_c = _dot(vct_ref[vrows, :].astype(BF16), p.astype(BF16))
        p_sum = p[:, :nq]
        for r in range(1, r_n):
            p_sum = p_sum + p[:, r * nq:(r + 1) * nq]

        imp = _dot(covt_ref[...], p_sum, precision=HIGHEST)
        cur = jnp.where(row < own, imp, -1.0)
        sel = _top_select_rows(cur, row, own, SLC_TOP)
        sel = jnp.where(row == own, 1.0, sel)
        selk_ref[...] = _dot(expt_ref[...], sel.astype(BF16))

        def slc_tile(state, k0, diag):
            kt = ksw_ref[pl.ds(k0, tk), 0:LANES].astype(BF16)
            vt = vst_ref[vrows, pl.ds(k0, tk)].astype(BF16)
            mk = selk_ref[pl.ds(k0, tk), :] > 0.5
            if diag:
                mk = mk & (k0 + krow <= tq)
            return _tflash_step(state, kt, qt, head_bias(pos_ref[pl.ds(k0, tk), :]), [vt], _lane_tile(mk, r_n))

        j_diag = i // r_n
        s_state = slc_tile(_tflash_init(d, n_all), pl.multiple_of(j_diag * tk, tk), True)
        _, l_s, acc_s = lax.fori_loop(0, j_diag, lambda j, st: slc_tile(st, pl.multiple_of(j * tk, tk), False),
                                      s_state)

        k0 = pl.multiple_of(jnp.maximum(t0 - WINDOW, 0), nq)
        wrow = k0 + lax.broadcasted_iota(jnp.int32, (WINDOW + nq, 1), 0)
        dist = tq - wrow
        kt = ksw_ref[pl.ds(k0, WINDOW + nq), LANES:2 * LANES].astype(BF16)
        vt = vwt_ref[vrows, pl.ds(k0, WINDOW + nq)].astype(BF16)
        wmask = _lane_tile((dist >= 0) & (dist <= WINDOW), r_n)
        _, l_w, acc_w = _tflash_step(_tflash_init(d, n_all), kt, qt, head_bias(pos_ref[pl.ds(k0, WINDOW + nq), :]),
                                     [vt], wmask)

        o_s = acc_s / l_s
        o_w = acc_w / l_w
        for r, h in enumerate(heads):
            seg = slice(r * nq, (r + 1) * nq)
            head_out[h] = (gates_t[h:h + 1] * o_c[:, seg]
                           + gates_t[NSA_HEADS + h:NSA_HEADS + h + 1] * o_s[:, seg]
                           + gates_t[2 * NSA_HEADS + h:2 * NSA_HEADS + h + 1] * o_w[:, seg])

    for cb in range(n_cb):
        o_ref[:, cb * LANES:(cb + 1) * LANES] = jnp.concatenate([head_out[2 * cb], head_out[2 * cb + 1]], axis=0).T


def _nsa_attn(p_even, kc, vc, b, t):
    nq = LANES
    n_tiles = t // nq
    covt = _cover_matrix().T
    expt = _expand_matrix(SLC_BLOCK, t).T
    pos = _pos_lanes(t)
    return pl.pallas_call(
        _nsa_attn_kernel,
        grid=(b, n_tiles),
        in_specs=[
            pl.BlockSpec((nq, NSA_HEADS * HEAD_DIM), lambda bi, i: (bi * n_tiles + i, E_QN // 512)),
            pl.BlockSpec((nq, 2 * LANES), lambda bi, i: (bi * n_tiles + i, E_GT // 256)),
            pl.BlockSpec((t, 2 * LANES), lambda bi, i: (bi, E_KS // 256)),
            pl.BlockSpec((t, 2 * LANES), lambda bi, i: (bi, E_VS // 256)),
            pl.BlockSpec((1, LANES, LANES), lambda bi, i: (bi, 0, 0)),
            pl.BlockSpec((1, LANES, LANES), lambda bi, i: (bi, 0, 0)),
            pl.BlockSpec((LANES, LANES), lambda bi, i: (0, 0)),
            pl.BlockSpec((t, LANES), lambda bi, i: (0, 0)),
            pl.BlockSpec((t, LANES), lambda bi, i: (0, 0)),
        ],
        out_specs=pl.BlockSpec((nq, NSA_HEADS * HEAD_DIM), lambda bi, i: (bi * n_tiles + i, 0)),
        out_shape=jax.ShapeDtypeStruct((b * t, NSA_HEADS * HEAD_DIM), F32),
        scratch_shapes=[pltpu.VMEM((LANES, t), F32), pltpu.VMEM((LANES, t), F32), pltpu.VMEM((LANES, LANES), F32),
                        pltpu.VMEM((t, nq), F32)],
        compiler_params=_params("parallel", "arbitrary"),
        name="nsa_attn",
    )(p_even, p_even, p_even, p_even, kc, vc, covt, expt, pos)


def _moba_attn_kernel(q_ref, k_ref, v_ref, pos_ref, o_ref, km_ref, vt_ref):
    hp = pl.program_id(1)
    i = pl.program_id(2)
    nq = q_ref.shape[0]
    n_b = k_ref.shape[0] // MOBA_BLOCK
    d = HEAD_DIM

    @pl.when(i == 0)
    def _():
        for n in range(n_b):
            km_ref[n:n + 1, :] = jnp.mean(k_ref[n * MOBA_BLOCK:(n + 1) * MOBA_BLOCK, :], axis=0, keepdims=True)
        _transpose_into(vt_ref, v_ref, 0)

    bpt = 2
    tk = bpt * MOBA_BLOCK
    lane = lax.broadcasted_iota(jnp.int32, (1, nq), 1)
    row = lax.broadcasted_iota(jnp.int32, (LANES, 1), 0)
    krow = lax.broadcasted_iota(jnp.int32, (tk, 1), 0)
    brow = lax.broadcasted_iota(jnp.int32, (n_b, 1), 0)
    tq = i * nq + lane
    q_t = q_ref[...].T
    qts, sels, biases = [], [], []
    for hh in range(2):
        qh = jnp.where(row // d == hh, q_t, 0.0)
        route = _dot(km_ref[...], qh, precision=HIGHEST)
        cur = jnp.where(brow < i, route, NEG_INF)
        sel = _top_select_rows(cur, brow, i, MOBA_TOP)
        sels.append(jnp.where(brow == i, 1.0, sel))
        qts.append((qh * QK_SCALE).astype(BF16))
        biases.append(jnp.exp2(-(jnp.zeros((1, 1), jnp.int32) + (hp * 2 + hh + 1)).astype(F32)))
    qt = jnp.concatenate(qts, axis=1)

    def tile(state, j, diag):
        k0 = pl.multiple_of(j * tk, tk)
        kt = k_ref[pl.ds(k0, tk), :].astype(BF16)
        base = pos_ref[pl.ds(k0, tk), :]
        vts = [vt_ref[hh * d:(hh + 1) * d, pl.ds(k0, tk)].astype(BF16) for hh in range(2)]
        bias = jnp.concatenate([_lane_tile(biases[hh] * base, nq // LANES) for hh in range(2)], axis=1)
        masks = []
        for hh in range(2):
            chosen = [jnp.max(jnp.where(brow == j * bpt + bb, sels[hh], 0.0), axis=0, keepdims=True)
                      for bb in range(bpt)]
            mk = jnp.where(krow < MOBA_BLOCK, chosen[0], chosen[1]) > 0.5
            if diag:
                mk = mk & (k0 + krow <= tq)
            masks.append(mk)
        return _tflash_step(state, kt, qt, bias, vts, jnp.concatenate(masks, axis=1))

    j_diag = i // bpt
    state = tile(_tflash_init(d, 2 * nq), j_diag, True)
    _, l, acc = lax.fori_loop(0, j_diag, lambda j, st: tile(st, j, False), state)
    o = acc / l
    o_ref[...] = jnp.concatenate([o[:, :nq], o[:, nq:]], axis=0).T


def _moba_attn(p_even, b, t):
    nq = MOBA_BLOCK
    n_tiles = t // nq
    return pl.pallas_call(
        _moba_attn_kernel,
        grid=(b, MOBA_HEADS // 2, n_tiles),
        in_specs=[
            pl.BlockSpec((nq, LANES), lambda bi, hp, i: (bi * n_tiles + i, E_QM // LANES + hp)),
            pl.BlockSpec((t, LANES), lambda bi, hp, i: (bi, E_KM // LANES + hp)),
            pl.BlockSpec((t, LANES), lambda bi, hp, i: (bi, E_VM // LANES + hp)),
            pl.BlockSpec((t, LANES), lambda bi, hp, i: (0, 0)),
        ],
        out_specs=pl.BlockSpec((nq, LANES), lambda bi, hp, i: (bi * n_tiles + i, hp)),
        out_shape=jax.ShapeDtypeStruct((b * t, MOBA_HEADS * HEAD_DIM), F32),
        scratch_shapes=[pltpu.VMEM((t // MOBA_BLOCK, LANES), F32), pltpu.VMEM((LANES, t), F32)],
        compiler_params=_params("parallel", "parallel", "arbitrary"),
        name="moba_attn",
    )(p_even, p_even, p_even, _pos_lanes(t))


def _diff_attn_kernel(lambda_init, q_ref, k_ref, v_ref, pos_ref, lam_ref, sg_ref, o_ref, vt_ref):
    h = pl.program_id(1)
    i = pl.program_id(2)
    nq = q_ref.shape[0]
    d = HEAD_DIM

    @pl.when(i == 0)
    def _():
        _transpose_into(vt_ref, v_ref, 0)

    tk = 2 * nq
    lane = lax.broadcasted_iota(jnp.int32, (1, nq), 1)
    row = lax.broadcasted_iota(jnp.int32, (LANES, 1), 0)
    krow = lax.broadcasted_iota(jnp.int32, (tk, 1), 0)
    tq = i * nq + lane
    slope = jnp.exp2(-(jnp.zeros((1, 1), jnp.int32) + (h + 1)).astype(F32))
    q_t = q_ref[...].T * QK_SCALE
    qt = jnp.concatenate([jnp.where(row < d, q_t, 0.0), jnp.where(row >= d, q_t, 0.0)], axis=1).astype(BF16)

    def tile(state, j, diag):
        k0 = pl.multiple_of(j * tk, tk)
        kt = k_ref[pl.ds(k0, tk), :].astype(BF16)
        vt = vt_ref[:, pl.ds(k0, tk)].astype(BF16)
        bias = _lane_tile(slope * pos_ref[pl.ds(k0, tk), :], 2 * nq // LANES)
        mask = _lane_tile(k0 + krow <= tq, 2) if diag else None
        return _tflash_step(state, kt, qt, bias, [vt], mask)

    j_diag = i * nq // tk
    state = tile(_tflash_init(2 * d, 2 * nq), j_diag, True)
    _, l, acc = lax.fori_loop(0, j_diag, lambda j, st: tile(st, j, False), state)
    o = acc / l
    attn_t = o[:, :nq] - _diff_lambda(lam_ref, lambda_init) * o[:, nq:]
    o_ref[...] = _rms(attn_t.T, sg_ref[...]) * (1.0 - lambda_init)


def _diff_attn(p_odd, lam_rows, subln_gain, lambda_init, b, t):
    nq = 256
    n_tiles = t // nq
    d2 = 2 * HEAD_DIM
    return pl.pallas_call(
        functools.partial(_diff_attn_kernel, lambda_init),
        grid=(b, DIFF_HEADS, n_tiles),
        in_specs=[
            pl.BlockSpec((nq, d2), lambda bi, h, i: (bi * n_tiles + i, O_Q // d2 + h)),
            pl.BlockSpec((t, d2), lambda bi, h, i: (bi, O_K // d2 + h)),
            pl.BlockSpec((t, d2), lambda bi, h, i: (bi, O_V // d2 + h)),
            pl.BlockSpec((t, LANES), lambda bi, h, i: (0, 0)),
            pl.BlockSpec(lam_rows.shape, lambda bi, h, i: (0, 0)),
            pl.BlockSpec((1, d2), lambda bi, h, i: (0, 0)),
        ],
        out_specs=pl.BlockSpec((nq, d2), lambda bi, h, i: (bi * n_tiles + i, h)),
        out_shape=jax.ShapeDtypeStruct((b * t, DIFF_HEADS * d2), F32),
        scratch_shapes=[pltpu.VMEM((d2, t), F32)],
        compiler_params=_params("parallel", "parallel", "arbitrary"),
        name="diff_attn",
    )(p_odd, p_odd, p_odd, _pos_lanes(t), lam_rows, subln_gain.reshape(1, d2))


def _decode_softmax(s, mask, s_new):
    sm = jnp.where(mask, s, NEG_INF)
    m = jnp.maximum(jnp.max(sm, axis=-1, keepdims=True), s_new)
    p = jnp.where(mask, jnp.exp(sm - m), 0.0)
    p_new = jnp.exp(s_new - m)
    l = jnp.sum(p, axis=-1, keepdims=True) + p_new
    return p, p_new, l


def _nsa_decode_kernel(n_pages, pt_ref, *refs):
    ks_pages = refs[:n_pages]
    vs_pages = refs[n_pages:2 * n_pages]
    (q_ref, gt_ref, kc_ref, vc_ref, wk_ref, wv_ref, new_ref, cov_ref, exp_ref, o_ref) = refs[2 * n_pages:]
    d = HEAD_DIM
    t = n_pages * PAGE_SIZE
    hrow = lax.broadcasted_iota(jnp.int32, (NSA_HEADS, 1), 0)
    grp0 = hrow < NSA_GROUP
    slope = jnp.exp2(-(hrow + 1).astype(F32))
    lane = lax.broadcasted_iota(jnp.int32, (1, LANES), 1)
    q64 = q_ref[0] * QK_SCALE
    q = jnp.where((lane >= d) == (hrow >= NSA_GROUP), jnp.concatenate([q64, q64], axis=1), 0.0)
    qb = q.astype(BF16)
    gates = jax.nn.sigmoid(gt_ref[0])
    new = new_ref[0]

    def own_group(o):
        return jnp.where(grp0, o[:, :d], o[:, d:])

    def new_row(r):
        return own_group(new[r:r + 1, :])

    c_end = lane * CMP_STRIDE + (CMP_BLOCK - 1)
    s = _nt_dot(qb, kc_ref[0].astype(BF16)) - slope * (t - c_end).astype(F32)
    valid = c_end <= t
    sm = jnp.where(valid, s, NEG_INF)
    m = jnp.max(sm, axis=-1, keepdims=True)
    p = jnp.where(valid, jnp.exp(sm - m), 0.0)
    p_c = p / jnp.sum(p, axis=-1, keepdims=True)
    o_c = own_group(_dot(p_c.astype(BF16), vc_ref[0].astype(BF16)))

    ps0 = jnp.sum(jnp.where(grp0, p_c, 0.0), axis=0, keepdims=True)
    ps1 = jnp.sum(jnp.where(grp0, 0.0, p_c), axis=0, keepdims=True)
    p_sum = jnp.where(grp0, ps0, ps1)
    imp = _dot(p_sum, cov_ref[...], precision=HIGHEST)
    own = t // SLC_BLOCK
    cur = jnp.where(lane < own, imp, -1.0)
    sel = _top_select(cur, lane, own, SLC_TOP)
    selk = _dot(sel.astype(BF16), exp_ref[...]) > 0.5

    ks = jnp.concatenate([pg[0] for pg in ks_pages], axis=1).astype(BF16)
    vs = jnp.concatenate([pg[0] for pg in vs_pages], axis=1).astype(BF16)
    pos = lax.broadcasted_iota(jnp.int32, (1, t), 1)
    s = _dot(qb, ks) - slope * (t - pos).astype(F32)
    s_new = jnp.sum(q * new[0:1, :], axis=-1, keepdims=True)
    p, p_new, l = _decode_softmax(s, selk, s_new)
    o_s = (own_group(_nt_dot(p.astype(BF16), vs)) + p_new * new_row(1)) / l

    wk = wk_ref[0].astype(BF16)
    wv = wv_ref[0].astype(BF16)
    w_n = wk.shape[1]
    wpos = (t - w_n) + lax.broadcasted_iota(jnp.int32, (1, w_n), 1)
    s = _dot(qb, wk) - slope * (t - wpos).astype(F32)
    s_new = jnp.sum(q * new[2:3, :], axis=-1, keepdims=True)
    p, p_new, l = _decode_softmax(s, (t - wpos) <= WINDOW, s_new)
    o_w = (own_group(_nt_dot(p.astype(BF16), wv)) + p_new * new_row(3)) / l

    o_ref[0] = gates[:, 0:1] * o_c + gates[:, 1:2] * o_s + gates[:, 2:3] * o_w


def _nsa_decode(page_table, ks_pool, vs_pool, q8, gates8, kc, vc, win_k, win_v, new_rows):
    b, n_pages = page_table.shape
    t = n_pages * PAGE_SIZE
    gd = NSA_KV_HEADS * HEAD_DIM
    page_specs = [pl.BlockSpec((1, gd, PAGE_SIZE), lambda i, pt, p=p: (pt[i * n_pages + p], 0, 0))
                  for p in range(n_pages)]
    per_b = lambda a: pl.BlockSpec((1,) + a.shape[1:], lambda i, pt: (i,) + (0,) * (a.ndim - 1))
    const = lambda a: pl.BlockSpec(a.shape, lambda i, pt: (0,) * a.ndim)
    cov = _cover_matrix()
    exp = _expand_matrix(SLC_BLOCK, t)
    per_sample = [q8, gates8, kc, vc, win_k, win_v, new_rows]
    return pl.pallas_call(
        functools.partial(_nsa_decode_kernel, n_pages),
        grid_spec=pltpu.PrefetchScalarGridSpec(
            num_scalar_prefetch=1,
            grid=(b,),
            in_specs=page_specs + page_specs + [per_b(a) for a in per_sample] + [const(cov), const(exp)],
            out_specs=pl.BlockSpec((1, NSA_HEADS, HEAD_DIM), lambda i, pt: (i, 0, 0)),
        ),
        out_shape=jax.ShapeDtypeStruct((b, NSA_HEADS, HEAD_DIM), F32),
        compiler_params=_params("parallel"),
        name="nsa_decode",
    )(page_table.reshape(-1), *([ks_pool] * n_pages), *([vs_pool] * n_pages), *per_sample, cov, exp)


def _moba_decode_kernel(n_pages, pt_ref, *refs):
    k_pages = refs[:n_pages]
    v_pages = refs[n_pages:2 * n_pages]
    q_ref, new_ref, exp_ref, o_ref = refs[2 * n_pages:]
    d = HEAD_DIM
    hd = MOBA_HEADS * d
    t = n_pages * PAGE_SIZE
    ppb = MOBA_BLOCK // PAGE_SIZE
    n_b = n_pages // ppb
    hrow = lax.broadcasted_iota(jnp.int32, (MOBA_HEADS, 1), 0)
    slope = jnp.exp2(-(hrow + 1).astype(F32))
    lane = lax.broadcasted_iota(jnp.int32, (1, LANES), 1)
    head_of_lane = lax.broadcasted_iota(jnp.int32, (1, hd), 1) // d
    own_head = head_of_lane == hrow
    q = jnp.where(own_head, q_ref[0], 0.0)
    new = new_ref[0]

    k_mean = jnp.zeros((hd, LANES), F32)
    for n in range(n_b):
        blk = k_pages[n * ppb][0]
        for pp in range(1, ppb):
            blk = blk + k_pages[n * ppb + pp][0]
        col = jnp.sum(blk, axis=-1, keepdims=True) * (1.0 / MOBA_BLOCK)
        k_mean = jnp.where(lane == n, col, k_mean)
    route = _dot(q, k_mean, precision=HIGHEST)
    cur = jnp.where(lane < n_b, route, -2e38)
    sel = _top_select(cur, lane, n_b, MOBA_TOP)
    selk = _dot(sel.astype(BF16), exp_ref[...]) > 0.5

    k_all = jnp.concatenate([pg[0] for pg in k_pages], axis=1).astype(BF16)
    v_all = jnp.concatenate([pg[0] for pg in v_pages], axis=1).astype(BF16)
    qs = q * QK_SCALE
    pos = lax.broadcasted_iota(jnp.int32, (1, t), 1)
    s = _dot(qs.astype(BF16), k_all) - slope * (t - pos).astype(F32)
    s_new = jnp.sum(qs * new[0:1], axis=-1, keepdims=True)
    p, p_new, l = _decode_softmax(s, selk, s_new)
    o = (_nt_dot(p.astype(BF16), v_all) + p_new * new[1:2]) / l
    o_ref[0] = jnp.sum(jnp.where(own_head, o, 0.0), axis=0, keepdims=True)


def _moba_decode(page_table, k_pool, v_pool, q_rows, new_rows):
    b, n_pages = page_table.shape
    t = n_pages * PAGE_SIZE
    hd = MOBA_HEADS * HEAD_DIM
    page_specs = [pl.BlockSpec((1, hd, PAGE_SIZE), lambda i, pt, p=p: (pt[i * n_pages + p], 0, 0))
                  for p in range(n_pages)]
    per_b = lambda a: pl.BlockSpec((1,) + a.shape[1:], lambda i, pt: (i,) + (0,) * (a.ndim - 1))
    exp = _expand_matrix(MOBA_BLOCK, t)
    return pl.pallas_call(
        functools.partial(_moba_decode_kernel, n_pages),
        grid_spec=pltpu.PrefetchScalarGridSpec(
            num_scalar_prefetch=1,
            grid=(b,),
            in_specs=page_specs + page_specs + [per_b(q_rows), per_b(new_rows),
                                                pl.BlockSpec(exp.shape, lambda i, pt: (0, 0))],
            out_specs=pl.BlockSpec((1, 1, hd), lambda i, pt: (i, 0, 0)),
        ),
        out_shape=jax.ShapeDtypeStruct((b, 1, hd), F32),
        compiler_params=_params("parallel"),
        name="moba_decode",
    )(page_table.reshape(-1), *([k_pool] * n_pages), *([v_pool] * n_pages), q_rows, new_rows, exp)


def _diff_decode_kernel(lambda_init, n_step, n_total, pt_ref, *refs):
    k_pages = refs[:n_step]
    v_pages = refs[n_step:2 * n_step]
    q_ref, knew_ref, vnew_ref, lam_ref, sg_ref, o_ref, m_ref, l_ref, acc_ref = refs[2 * n_step:]
    step = pl.program_id(1)
    d = HEAD_DIM
    nh = DIFF_HEADS
    t = n_total * PAGE_SIZE
    row = lax.broadcasted_iota(jnp.int32, (2 * nh, 1), 0)
    lane = lax.broadcasted_iota(jnp.int32, (1, 2 * d), 1)
    slope = jnp.exp2(-(row % nh + 1).astype(F32))
    q8 = q_ref[0] * QK_SCALE
    q = jnp.concatenate([jnp.where(lane < d, q8, 0.0), jnp.where(lane >= d, q8, 0.0)], axis=0)

    @pl.when(step == 0)
    def _():
        m_ref[...] = jnp.full(m_ref.shape, NEG_INF, F32)
        l_ref[...] = jnp.zeros_like(l_ref)
        acc_ref[...] = jnp.zeros_like(acc_ref)

    k_all = jnp.concatenate([pg[0].reshape(PAGE_SIZE * nh, 2 * d) for pg in k_pages], axis=0).astype(BF16)
    v_all = jnp.concatenate([pg[0].reshape(PAGE_SIZE * nh, 2 * d) for pg in v_pages], axis=0).astype(BF16)
    n_k = n_step * PAGE_SIZE * nh
    col = lax.broadcasted_iota(jnp.int32, (1, n_k), 1)
    pos = step * (n_step * PAGE_SIZE) + col // nh
    mask = col % nh == row % nh
    s = _nt_dot(q.astype(BF16), k_all) - slope * (t - pos).astype(F32)
    s = jnp.where(mask, s, NEG_INF)
    m_old = m_ref[:, 0:1]
    m_new = jnp.maximum(m_old, jnp.max(s, axis=-1, keepdims=True))
    alpha = jnp.exp(m_old - m_new)
    p = jnp.where(mask, jnp.exp(s - m_new), 0.0)
    l_new = alpha * l_ref[:, 0:1] + jnp.sum(p, axis=-1, keepdims=True)
    acc_new = alpha * acc_ref[...] + _dot(p.astype(BF16), v_all)
    m_ref[...] = jnp.broadcast_to(m_new, m_ref.shape)
    l_ref[...] = jnp.broadcast_to(l_new, l_ref.shape)
    acc_ref[...] = acc_new

    @pl.when(step == pl.num_programs(1) - 1)
    def _():
        k_new = jnp.concatenate([knew_ref[0]] * 2, axis=0)
        v_new = jnp.concatenate([vnew_ref[0]] * 2, axis=0)
        s_new = jnp.sum(q * k_new, axis=-1, keepdims=True)
        m_fin = jnp.maximum(m_new, s_new)
        a = jnp.exp(m_new - m_fin)
        p_new = jnp.exp(s_new - m_fin)
        o = (a * acc_new + p_new * v_new) / (a * l_new + p_new)
        lam = _diff_lambda(lam_ref, lambda_init)
        attn = o[:nh] - lam * o[nh:]
        o_ref[0] = _rms(attn, sg_ref[...]) * (1.0 - lambda_init)


def _diff_decode(page_table, k_pool, v_pool, q_rows, k_new, v_new, lam_rows, subln_gain, lambda_init):
    b, n_pages = page_table.shape
    nh, d2 = DIFF_HEADS, 2 * HEAD_DIM
    n_step = n_pages // 2
    page_specs = [pl.BlockSpec((1, PAGE_SIZE, nh, d2),
                               lambda i, s, pt, p=p: (pt[i * n_pages + s * n_step + p], 0, 0, 0))
                  for p in range(n_step)]
    per_b = pl.BlockSpec((1, nh, d2), lambda i, s, pt: (i, 0, 0))
    const = lambda a: pl.BlockSpec(a.shape, lambda i, s, pt: (0,) * a.ndim)
    sg = subln_gain.reshape(1, d2)
    return pl.pallas_call(
        functools.partial(_diff_decode_kernel, lambda_init, n_step, n_pages),
        grid_spec=pltpu.PrefetchScalarGridSpec(
            num_scalar_prefetch=1,
            grid=(b, n_pages // n_step),
            in_specs=page_specs + page_specs + [per_b, per_b, per_b, const(lam_rows), const(sg)],
            out_specs=pl.BlockSpec((1, nh, d2), lambda i, s, pt: (i, 0, 0)),
            scratch_shapes=[pltpu.VMEM((2 * nh, LANES), F32), pltpu.VMEM((2 * nh, LANES), F32),
                            pltpu.VMEM((2 * nh, d2), F32)],
        ),
        out_shape=jax.ShapeDtypeStruct((b, nh, d2), F32),
        compiler_params=_params("parallel", "arbitrary"),
        name="diff_decode",
    )(page_table.reshape(-1), *([k_pool] * n_step), *([v_pool] * n_step), q_rows, k_new, v_new, lam_rows, sg)


def _even_weights(w_in, w_out, q_gain, ks_gain, kw_gain, mq_gain, mk_gain):
    d, g_n, ha, hb = HEAD_DIM, NSA_KV_HEADS, NSA_HEADS, MOBA_HEADS
    sizes = [ha * d] + [g_n * d] * 6 + [3 * ha, hb * d, hb * d, hb * d]
    offs = np.concatenate([[0], np.cumsum(sizes)])
    q_n, kc, vc, ks, vs, kw, vw, gates, q_m, k_m, v_m = [w_in[:, offs[i]:offs[i + 1]] for i in range(11)]
    gates = gates.reshape(-1, ha, 3).transpose(0, 2, 1).reshape(-1, 3 * ha)
    pad = jnp.zeros((w_in.shape[0], E_COLS - E_GT - 3 * ha), w_in.dtype)
    w = jnp.concatenate([q_n, ks, kw, q_m, k_m, kc, vc, vs, vw, v_m, gates, pad], axis=1)
    ones = jnp.ones((E_COLS - E_KC,), F32)
    col_gain = jnp.concatenate([jnp.tile(q_gain, ha), jnp.tile(ks_gain, g_n), jnp.tile(kw_gain, g_n),
                                jnp.tile(mq_gain, hb), jnp.tile(mk_gain, hb), ones])
    w_out = w_out.astype(BF16)
    return w, col_gain, w_out[:ha * d], w_out[ha * d:]


def _odd_weights(w_in, w_out, q_gain, k_gain):
    n_half = 2 * DIFF_HEADS
    col_gain = jnp.concatenate([jnp.tile(q_gain, n_half), jnp.tile(k_gain, n_half),
                                jnp.ones((O_COLS - O_V,), F32)])
    return w_in.astype(BF16), col_gain, w_out.astype(BF16)


def _trunk_prompt(x, prm):
    b, t, dm = x.shape
    d, g_n = HEAD_DIM, NSA_KV_HEADS
    gd = g_n * d
    n_pages = t // PAGE_SIZE
    x = x.reshape(b * t, dm)
    x = _ffn(x, prm["ffn1_norm"][0], prm["ffn1_w_in"][0], prm["ffn1_w_out"][0])
    pe = _proj(x, prm["mix_norm"][0], prm["even_w"], prm["even_gain"], E_FLAGS)
    col = lambda off, width: pe[:, off:off + width]
    kc_rows, vc_rows = col(E_KC, gd), col(E_VC, gd)
    kc, vc = _compress_prompt(pe, prm["cmp_k"], prm["cmp_v"], prm["nsa_kc_gain"], b, t)
    o_nsa = _nsa_attn(pe, kc, vc, b, t)
    o_moba = _moba_attn(pe, b, t)
    x = _oproj(x, [o_nsa, o_moba], [prm["even_wo_nsa"], prm["even_wo_moba"]])
    x = _ffn(x, prm["ffn2_norm"][0], prm["ffn2_w_in"][0], prm["ffn2_w_out"][0])
    n_keep = min(WINDOW, t)
    last = lambda a: a.reshape(b, t, g_n, d)[:, t - n_keep:]
    even = (kc_rows.reshape(b, t, g_n, d), vc_rows.reshape(b, t, g_n, d),
            col(E_KS, gd).reshape(b, t, g_n, d), col(E_VS, gd).reshape(b, t, g_n, d),
            col(E_KM, MOBA_HEADS * d).reshape(b, t, MOBA_HEADS, d),
            col(E_VM, MOBA_HEADS * d).reshape(b, t, MOBA_HEADS, d),
            last(col(E_KW, gd)), last(col(E_VW, gd)))
    x = _ffn(x, prm["ffn1_norm"][1], prm["ffn1_w_in"][1], prm["ffn1_w_out"][1])
    po = _proj(x, prm["mix_norm"][1], prm["odd_w"], prm["odd_gain"], O_FLAGS)
    o_diff = _diff_attn(po, prm["lam_rows"], prm["subln_gain"], prm["lambda_init"], b, t)
    x = _oproj(x, [o_diff], [prm["odd_wo"]])
    x = _ffn(x, prm["ffn2_norm"][1], prm["ffn2_w_in"][1], prm["ffn2_w_out"][1])
    odd = (po[:, O_K:O_V].reshape(b, t, DIFF_HEADS, 2 * d), po[:, O_V:].reshape(b, t, DIFF_HEADS, 2 * d))
    return x.reshape(b, t, dm), even, odd


def _trunk_sample(x, prm, past):
    b, t, dm = x.shape
    d, g_n = HEAD_DIM, NSA_KV_HEADS
    gd = g_n * d
    hm = MOBA_HEADS * d
    pt = past["page_table"]
    x = x.reshape(b, dm)
    x = _ffn(x, prm["ffn1_norm"][0], prm["ffn1_w_in32"][0], prm["ffn1_w_out32"][0])
    pe = _proj(x, prm["mix_norm"][0], prm["even_w32"], prm["even_gain"], E_FLAGS)
    col = lambda off, width: pe[:, off:off + width]
    pool_t = lambda c: jnp.transpose(c, (0, 2, 3, 1)).reshape(c.shape[0], -1, c.shape[1])
    kc, vc = _compress_decode(pt, pool_t(past["cmp_k"]), pool_t(past["cmp_v"]), prm["cmp_k"], prm["cmp_v"],
                              prm["nsa_kc_gain"])
    q8 = col(E_QN, NSA_HEADS * d).reshape(b, NSA_HEADS, d)
    gates = col(E_GT, 3 * NSA_HEADS).reshape(b, 3, NSA_HEADS).transpose(0, 2, 1)
    gates8 = jnp.pad(gates, ((0, 0), (0, 0), (0, LANES - 3)))
    new_nsa = jnp.stack([col(E_KS, gd), col(E_VS, gd), col(E_KW, gd), col(E_VW, gd)], axis=1)
    new_nsa = jnp.pad(new_nsa, ((0, 0), (0, 4), (0, 0)))
    o_nsa = _nsa_decode(pt, pool_t(past["slc_k"]), pool_t(past["slc_v"]), q8, gates8, kc, vc,
                        pool_t(past["win_k"]), pool_t(past["win_v"]), new_nsa).reshape(b, NSA_HEADS * d)
    new_moba = jnp.pad(jnp.stack([col(E_KM, hm), col(E_VM, hm)], axis=1), ((0, 0), (0, 6), (0, 0)))
    o_moba = _moba_decode(pt, pool_t(past["moba_k"]), pool_t(past["moba_v"]),
                          col(E_QM, hm).reshape(b, 1, hm), new_moba).reshape(b, hm)
    x = _oproj(x, [o_nsa, o_moba], [prm["even_wo_nsa"], prm["even_wo_moba"]])
    x = _ffn(x, prm["ffn2_norm"][0], prm["ffn2_w_in"][0], prm["ffn2_w_out"][0])
    roll = lambda win, new: jnp.concatenate([win[:, 1:], new.reshape(b, 1, g_n, d)], axis=1)
    r4 = lambda a, h: a.reshape(b, 1, h, a.shape[-1] // h)
    even = (r4(col(E_KC, gd), g_n), r4(col(E_VC, gd), g_n), r4(col(E_KS, gd), g_n), r4(col(E_VS, gd), g_n),
            r4(col(E_KM, hm), MOBA_HEADS), r4(col(E_VM, hm), MOBA_HEADS),
            roll(past["win_k"], col(E_KW, gd)), roll(past["win_v"], col(E_VW, gd)))
    x = _ffn(x, prm["ffn1_norm"][1], prm["ffn1_w_in"][1], prm["ffn1_w_out"][1])
    po = _proj(x, prm["mix_norm"][1], prm["odd_w"], prm["odd_gain"], O_FLAGS)
    heads = lambda a: a.reshape(b, DIFF_HEADS, 2 * d)
    o_diff = _diff_decode(pt, past["diff_k"], past["diff_v"], heads(po[:, :O_K]), heads(po[:, O_K:O_V]),
                          heads(po[:, O_V:]), prm["lam_rows"], prm["subln_gain"],
                          prm["lambda_init"]).reshape(b, DIFF_HEADS * 2 * d)
    x = _oproj(x, [o_diff], [prm["odd_wo"]])
    x = _ffn(x, prm["ffn2_norm"][1], prm["ffn2_w_in"][1], prm["ffn2_w_out"][1])
    odd = (r4(po[:, O_K:O_V], DIFF_HEADS), r4(po[:, O_V:], DIFF_HEADS))
    return x.reshape(b, t, dm), even, odd


def kernel(x_prompt, x_sample, cache_nsa_cmp_k, cache_nsa_cmp_v, cache_nsa_slc_k, cache_nsa_slc_v,
           cache_moba_k, cache_moba_v, state_nsa_win_k, state_nsa_win_v, cache_diff_k, cache_diff_v,
           page_table, ffn1_norm, ffn1_w_in, ffn1_w_out, mix_norm, ffn2_norm, ffn2_w_in, ffn2_w_out,
           even_w_in, even_w_out, nsa_q_gain, nsa_kc_gain, nsa_ks_gain, nsa_kw_gain,
           cmp_k_pe, cmp_k_w1, cmp_k_w2, cmp_v_pe, cmp_v_w1, cmp_v_w2, moba_q_gain, moba_k_gain,
           odd_w_in, odd_w_out, diff_q_gain, diff_k_gain, diff_lq1, diff_lk1, diff_lq2, diff_lk2,
           diff_subln_gain):
    even_w, even_gain, wo_nsa, wo_moba = _even_weights(even_w_in[0], even_w_out[0], nsa_q_gain[0], nsa_ks_gain[0],
                                                       nsa_kw_gain[0], moba_q_gain[0], moba_k_gain[0])
    odd_w, odd_gain, odd_wo = _odd_weights(odd_w_in[0], odd_w_out[0], diff_q_gain[0], diff_k_gain[0])
    lam_rows = jnp.concatenate([diff_lq1, diff_lk1, diff_lq2, diff_lk2, jnp.zeros((4, HEAD_DIM), F32)], axis=0)
    prm = {
        "ffn1_norm": ffn1_norm, "ffn1_w_in": ffn1_w_in.astype(BF16), "ffn1_w_out": ffn1_w_out.astype(BF16),
        "ffn2_norm": ffn2_norm, "ffn2_w_in": ffn2_w_in.astype(BF16), "ffn2_w_out": ffn2_w_out.astype(BF16),
        "mix_norm": mix_norm,
        "even_w": even_w.astype(BF16), "even_gain": even_gain, "even_wo_nsa": wo_nsa, "even_wo_moba": wo_moba,
        "even_w32": even_w, "ffn1_w_in32": ffn1_w_in, "ffn1_w_out32": ffn1_w_out,
        "cmp_k": _cmp_weights(cmp_k_pe[0], cmp_k_w1[0], cmp_k_w2[0]),
        "cmp_v": _cmp_weights(cmp_v_pe[0], cmp_v_w1[0], cmp_v_w2[0]),
        "nsa_kc_gain": nsa_kc_gain[0],
        "odd_w": odd_w, "odd_gain": odd_gain, "odd_wo": odd_wo,
        "lam_rows": lam_rows, "subln_gain": diff_subln_gain[0],
        "lambda_init": 0.8 - 0.6 * math.exp(-0.3 * 1),
    }
    past = {
        "page_table": page_table,
        "cmp_k": cache_nsa_cmp_k[0], "cmp_v": cache_nsa_cmp_v[0],
        "slc_k": cache_nsa_slc_k[0], "slc_v": cache_nsa_slc_v[0],
        "moba_k": cache_moba_k[0], "moba_v": cache_moba_v[0],
        "win_k": state_nsa_win_k[0], "win_v": state_nsa_win_v[0],
        "diff_k": cache_diff_k[0], "diff_v": cache_diff_v[0],
    }
    y_p, even_p, odd_p = _trunk_prompt(x_prompt, prm)
    y_s, even_s, odd_s = _trunk_sample(x_sample, prm, past)
    outs = [y_p, y_s]
    for p_leaf, s_leaf in zip(even_p + odd_p, even_s + odd_s):
        outs += [p_leaf[None], s_leaf[None]]
    return tuple(outs)
```

```python
import functools
import math

import numpy as np
import jax
import jax.numpy as jnp
from jax import lax
from jax.experimental import pallas as pl
from jax.experimental.pallas import tpu as pltpu

F32 = jnp.float32
BF16 = jnp.bfloat16
HIGHEST = lax.Precision.HIGHEST

D_MODEL = 1024
HEAD_DIM = 64
NSA_HEADS = 8
NSA_KV_HEADS = 2
NSA_GROUP = NSA_HEADS // NSA_KV_HEADS
CMP_BLOCK = 32
CMP_STRIDE = 16
CMP_HIDDEN = 2 * HEAD_DIM
SLC_BLOCK = 64
SLC_TOP = 3
WINDOW = 256
MOBA_HEADS = 8
MOBA_BLOCK = 256
MOBA_TOP = 3
DIFF_HEADS = D_MODEL // (2 * HEAD_DIM)
PAGE_SIZE = 128
RMS_EPS = 1e-6
NEG_INF = -1e30
QK_SCALE = HEAD_DIM ** -0.5
LANES = 128
MXU_N = 256
VMEM_LIMIT = 56 * 1024 * 1024

E_QN, E_KS, E_KW, E_QM, E_KM = 0, 512, 640, 768, 1280
E_KC, E_VC, E_VS, E_VW, E_VM, E_GT = 1792, 1920, 2048, 2176, 2304, 2816
E_COLS = 3072
E_FLAGS = (1, 1, 1, 1, 1, 1, 1, 0, 0, 0, 0, 0)
O_Q, O_K, O_V = 0, 1024, 2048
O_COLS = 3072
O_FLAGS = (1, 1, 1, 1, 1, 1, 1, 1, 0, 0, 0, 0)


def _nt_dot(a, b, precision=None):
    return lax.dot_general(a, b, (((1,), (1,)), ((), ())), precision=precision,
                           preferred_element_type=F32)


def _dot(a, b, precision=None):
    return jnp.dot(a, b, precision=precision, preferred_element_type=F32)


def _mm(a, w):
    if w.dtype == F32:
        return _dot(a.astype(F32), w, precision=HIGHEST)
    return _dot(a.astype(w.dtype), w)


def _rms(x, gain):
    return x * lax.rsqrt(jnp.mean(x * x, axis=-1, keepdims=True) + RMS_EPS) * gain


def _params(*sem):
    return pltpu.CompilerParams(dimension_semantics=sem, vmem_limit_bytes=VMEM_LIMIT)


def _ffn_kernel(x_ref, g_ref, wg_ref, wu_ref, wo_ref, o_ref, xn_ref, acc_ref):
    j = pl.program_id(1)

    @pl.when(j == 0)
    def _():
        xn_ref[...] = _rms(x_ref[...], g_ref[...]).astype(xn_ref.dtype)
        acc_ref[...] = jnp.zeros_like(acc_ref)

    xn = xn_ref[...]
    gate = _mm(xn, wg_ref[...])
    up = _mm(xn, wu_ref[...])
    acc_ref[...] += _mm(gate * jax.nn.sigmoid(gate) * up, wo_ref[...])

    @pl.when(j == pl.num_programs(1) - 1)
    def _():
        o_ref[...] = x_ref[...] + 0.5 * acc_ref[...]


def _ffn(x, gain, w_in, w_out):
    m, d = x.shape
    f = w_out.shape[0]
    tm = min(m, 512)
    tf = f // 2
    nf = f // tf
    return pl.pallas_call(
        _ffn_kernel,
        grid=(m // tm, nf),
        in_specs=[
            pl.BlockSpec((tm, d), lambda i, j: (i, 0)),
            pl.BlockSpec((1, d), lambda i, j: (0, 0)),
            pl.BlockSpec((d, tf), lambda i, j: (0, j)),
            pl.BlockSpec((d, tf), lambda i, j: (0, j + nf)),
            pl.BlockSpec((tf, d), lambda i, j: (j, 0)),
        ],
        out_specs=pl.BlockSpec((tm, d), lambda i, j: (i, 0)),
        out_shape=jax.ShapeDtypeStruct((m, d), F32),
        scratch_shapes=[pltpu.VMEM((tm, d), w_in.dtype), pltpu.VMEM((tm, d), F32)],
        compiler_params=_params("parallel", "arbitrary"),
        name="ffn",
    )(x, gain.reshape(1, d), w_in, w_in, w_out)


def _proj_kernel(flag_ref, x_ref, g_ref, w_ref, cg_ref, bd_ref, o_ref, xn_ref):
    j = pl.program_id(1)

    @pl.when(j == 0)
    def _():
        xn_ref[...] = _rms(x_ref[...], g_ref[...]).astype(xn_ref.dtype)

    y = _mm(xn_ref[...], w_ref[...])

    @pl.when(flag_ref[j] == 1)
    def _():
        y2 = y * y
        hi = y2.astype(BF16)
        lo = (y2 - hi.astype(F32)).astype(BF16)
        ss = _dot(hi, bd_ref[...]) + _dot(lo, bd_ref[...])
        o_ref[...] = y * lax.rsqrt(ss * (1.0 / HEAD_DIM) + RMS_EPS) * cg_ref[...]

    @pl.when(flag_ref[j] == 0)
    def _():
        o_ref[...] = y


def _proj(x, gain, w, col_gain, flags):
    m, d = x.shape
    n = w.shape[1]
    tm = min(m, 1024)
    tn = MXU_N
    head = np.arange(tn) // HEAD_DIM
    bd = jnp.asarray(head[:, None] == head[None, :], BF16)
    return pl.pallas_call(
        _proj_kernel,
        grid_spec=pltpu.PrefetchScalarGridSpec(
            num_scalar_prefetch=1,
            grid=(m // tm, n // tn),
            in_specs=[
                pl.BlockSpec((tm, d), lambda i, j, fl: (i, 0)),
                pl.BlockSpec((1, d), lambda i, j, fl: (0, 0)),
                pl.BlockSpec((d, tn), lambda i, j, fl: (0, j)),
                pl.BlockSpec((1, tn), lambda i, j, fl: (0, j)),
                pl.BlockSpec((tn, tn), lambda i, j, fl: (0, 0)),
            ],
            out_specs=pl.BlockSpec((tm, tn), lambda i, j, fl: (i, j)),
            scratch_shapes=[pltpu.VMEM((tm, d), w.dtype)],
        ),
        out_shape=jax.ShapeDtypeStruct((m, n), F32),
        compiler_params=_params("parallel", "arbitrary"),
        name="proj",
    )(jnp.asarray(flags, jnp.int32), x, gain.reshape(1, d), w, col_gain.reshape(1, n), bd)


def _oproj_kernel(n_in, x_ref, *refs):
    o_ref = refs[2 * n_in]
    acc = x_ref[...]
    for a_ref, w_ref in zip(refs[:n_in], refs[n_in:2 * n_in]):
        acc = acc + _dot(a_ref[...].astype(BF16), w_ref[...])
    o_ref[...] = acc


def _oproj(x, acts, ws):
    m, d = x.shape
    tm = min(m, 512)
    n_in = len(acts)
    in_specs = [pl.BlockSpec((tm, d), lambda i: (i, 0))]
    in_specs += [pl.BlockSpec((tm, a.shape[1]), lambda i: (i, 0)) for a in acts]
    in_specs += [pl.BlockSpec(w.shape, lambda i: (0, 0)) for w in ws]
    return pl.pallas_call(
        functools.partial(_oproj_kernel, n_in),
        grid=(m // tm,),
        in_specs=in_specs,
        out_specs=pl.BlockSpec((tm, d), lambda i: (i, 0)),
        out_shape=jax.ShapeDtypeStruct((m, d), F32),
        compiler_params=_params("parallel"),
        name="oproj",
    )(x, *acts, *ws)


def _top_select(cur, lane, own, n_top):
    sel = jnp.zeros(cur.shape, F32)
    for _ in range(n_top):
        mx = jnp.max(cur, axis=-1, keepdims=True)
        idx = jnp.min(jnp.where(cur == mx, lane, 4 * LANES), axis=-1, keepdims=True)
        pick = lane == idx
        sel = jnp.where(pick, jnp.where(lane < own, 1.0, 0.0), sel)
        cur = jnp.where(pick, -3e38, cur)
    return sel


def _softmax_update(carry, s, mask, v):
    m, l, acc = carry
    s = jnp.where(mask, s, NEG_INF)
    m_new = jnp.maximum(m, jnp.max(s, axis=-1, keepdims=True))
    alpha = jnp.exp(m - m_new)
    p = jnp.where(mask, jnp.exp(s - m_new), 0.0)
    l = alpha * l + jnp.sum(p, axis=-1, keepdims=True)
    acc = alpha * acc + _dot(p.astype(BF16), v)
    return m_new, l, acc


def _softmax_init(rows, width):
    return (jnp.full((rows, 1), NEG_INF, F32), jnp.zeros((rows, 1), F32), jnp.zeros((rows, width), F32))


def _safe_div(acc, l):
    return jnp.where(l > 0.0, acc / jnp.where(l > 0.0, l, 1.0), 0.0)


def _compress_rows(scr, n_c, w1_ref, w2_ref, pe_ref):
    x = jnp.concatenate([scr[pl.ds(r, n_c, stride=CMP_STRIDE), :] for r in range(CMP_BLOCK)], axis=1)
    hid = _dot(x.astype(BF16), w1_ref[...]) + _dot(pe_ref[...].astype(BF16), w1_ref[...])[0:1]
    hid = hid * jax.nn.sigmoid(hid)
    return _dot(hid.astype(BF16), w2_ref[...])


def _store_kc(kc, gk_ref, kc_ref):
    g = gk_ref[...]
    for grp in range(NSA_KV_HEADS):
        sl = slice(grp * HEAD_DIM, (grp + 1) * HEAD_DIM)
        kc_ref[0, :, sl] = _rms(kc[:, sl], g[:, sl])


def _cmp_prompt_kernel(k_ref, v_ref, w1k, w2k, pek, gk, w1v, w2v, pev, kc_ref, vc_ref, scr):
    t = k_ref.shape[0]
    n_c = t // CMP_STRIDE
    scr[t:, :] = jnp.zeros((scr.shape[0] - t, scr.shape[1]), F32)
    scr[:t, :] = k_ref[...]
    _store_kc(_compress_rows(scr, n_c, w1k, w2k, pek), gk, kc_ref)
    scr[:t, :] = v_ref[...]
    vc_ref[0] = _compress_rows(scr, n_c, w1v, w2v, pev)


def _cmp_decode_kernel(n_pages, pt_ref, *refs):
    k_pages = refs[:n_pages]
    v_pages = refs[n_pages:2 * n_pages]
    (w1k, w2k, pek, gk, w1v, w2v, pev, kc_ref, vc_ref, scr) = refs[2 * n_pages:]
    t = n_pages * PAGE_SIZE
    n_c = t // CMP_STRIDE
    scr[t:, :] = jnp.zeros((scr.shape[0] - t, scr.shape[1]), F32)
    for p, pg in enumerate(k_pages):
        scr[p * PAGE_SIZE:(p + 1) * PAGE_SIZE, :] = pg[0].T
    _store_kc(_compress_rows(scr, n_c, w1k, w2k, pek), gk, kc_ref)
    for p, pg in enumerate(v_pages):
        scr[p * PAGE_SIZE:(p + 1) * PAGE_SIZE, :] = pg[0].T
    vc_ref[0] = _compress_rows(scr, n_c, w1v, w2v, pev)


def _cmp_weights(pe, w1, w2):
    d, hdn, g_n = HEAD_DIM, CMP_HIDDEN, NSA_KV_HEADS
    w1r = w1.reshape(CMP_BLOCK, d, hdn)
    eye = jnp.eye(g_n, dtype=F32)
    w1b = jnp.einsum("rdk,gq->rgdqk", w1r, eye).reshape(CMP_BLOCK * g_n * d, g_n * hdn)
    w2b = jnp.einsum("kd,gq->gkqd", w2, eye).reshape(g_n * hdn, g_n * d)
    per = jnp.broadcast_to(pe.reshape(CMP_BLOCK, 1, d), (CMP_BLOCK, g_n, d)).reshape(1, -1)
    pe8 = jnp.concatenate([per, jnp.zeros((7, per.shape[1]), F32)], axis=0)
    return w1b.astype(BF16), w2b.astype(BF16), pe8


def _cmp_consts(wk, wv, kc_gain):
    gk = jnp.tile(kc_gain, NSA_KV_HEADS).reshape(1, NSA_KV_HEADS * HEAD_DIM)
    return [*wk, gk, *wv]


def _compress_prompt(p_even, wk, wv, kc_gain, b, t):
    n_c = t // CMP_STRIDE
    gd = NSA_KV_HEADS * HEAD_DIM
    consts = _cmp_consts(wk, wv, kc_gain)
    return pl.pallas_call(
        _cmp_prompt_kernel,
        grid=(b,),
        in_specs=[pl.BlockSpec((t, gd), lambda i: (i, E_KC // gd)), pl.BlockSpec((t, gd), lambda i: (i, E_VC // gd))]
        + [pl.BlockSpec(a.shape, lambda i: (0, 0)) for a in consts],
        out_specs=[pl.BlockSpec((1, n_c, gd), lambda i: (i, 0, 0))] * 2,
        out_shape=[jax.ShapeDtypeStruct((b, n_c, gd), F32)] * 2,
        scratch_shapes=[pltpu.VMEM((t + CMP_STRIDE, gd), F32)],
        compiler_params=_params("parallel"),
        name="nsa_compress_prompt",
    )(p_even, p_even, *consts)


def _compress_decode(page_table, k_pool_t, v_pool_t, wk, wv, kc_gain):
    b, n_pages = page_table.shape
    t = n_pages * PAGE_SIZE
    n_c = t // CMP_STRIDE
    gd = NSA_KV_HEADS * HEAD_DIM
    page_specs = [pl.BlockSpec((1, gd, PAGE_SIZE), lambda i, pt, p=p: (pt[i * n_pages + p], 0, 0))
                  for p in range(n_pages)]
    consts = _cmp_consts(wk, wv, kc_gain)
    return pl.pallas_call(
        functools.partial(_cmp_decode_kernel, n_pages),
        grid_spec=pltpu.PrefetchScalarGridSpec(
            num_scalar_prefetch=1,
            grid=(b,),
            in_specs=page_specs + page_specs + [pl.BlockSpec(a.shape, lambda i, pt: (0, 0)) for a in consts],
            out_specs=[pl.BlockSpec((1, n_c, gd), lambda i, pt: (i, 0, 0))] * 2,
            scratch_shapes=[pltpu.VMEM((t + CMP_STRIDE, gd), F32)],
        ),
        out_shape=[jax.ShapeDtypeStruct((b, n_c, gd), F32)] * 2,
        compiler_params=_params("parallel"),
        name="nsa_compress_decode",
    )(page_table.reshape(-1), *([k_pool_t] * n_pages), *([v_pool_t] * n_pages), *consts)


def _cover_matrix():
    j = np.arange(LANES)[:, None]
    n = np.arange(LANES)[None, :]
    c0 = j * CMP_STRIDE
    s0 = n * SLC_BLOCK
    return jnp.asarray((c0 < s0 + SLC_BLOCK) & (c0 + CMP_BLOCK - 1 >= s0), F32)


def _expand_matrix(block, t):
    n = np.arange(LANES)[:, None]
    s = np.arange(t)[None, :]
    return jnp.asarray(s // block == n, BF16)


def _nsa_kernel(q_ref, gt_ref, ksw_ref, vsw_ref, kc_ref, vc_ref, cov_ref, exp_ref, o_ref, selk_ref):
    i = pl.program_id(1)
    tq_n = q_ref.shape[0]
    r_n = NSA_GROUP
    d = HEAD_DIM
    t0 = i * tq_n
    rows = r_n * tq_n
    row = lax.broadcasted_iota(jnp.int32, (rows, 1), 0)
    tq = t0 + row % tq_n
    tq1 = tq[:tq_n]
    lane = lax.broadcasted_iota(jnp.int32, (1, LANES), 1)
    gates = jax.nn.sigmoid(gt_ref[:, :LANES])
    own = tq1 // SLC_BLOCK

    for g in range(NSA_KV_HEADS):
        kcol = slice(g * d, (g + 1) * d)
        wcol = slice(LANES + g * d, LANES + (g + 1) * d)
        qg = jnp.concatenate([q_ref[:, (g * r_n + r) * d:(g * r_n + r + 1) * d] for r in range(r_n)], axis=0)
        qg = (qg * QK_SCALE).astype(BF16)
        slope = jnp.exp2(-(g * r_n + row // tq_n + 1).astype(F32))

        c_end = lane * CMP_STRIDE + (CMP_BLOCK - 1)
        s = _nt_dot(qg, kc_ref[0, :, kcol].astype(BF16)) - slope * (tq - c_end).astype(F32)
        valid = c_end <= tq
        s = jnp.where(valid, s, NEG_INF)
        p_c = jnp.where(valid, jnp.exp(s - jnp.max(s, axis=-1, keepdims=True)), 0.0)
        p_c = _safe_div(p_c, jnp.sum(p_c, axis=-1, keepdims=True))
        o_c = _dot(p_c.astype(BF16), vc_ref[0, :, kcol].astype(BF16))

        p_sum = p_c[0:tq_n]
        for r in range(1, r_n):
            p_sum = p_sum + p_c[r * tq_n:(r + 1) * tq_n]
        imp = _dot(p_sum, cov_ref[...], precision=HIGHEST)
        cur = jnp.where(lane < own, imp, -1.0)
        sel = _top_select(cur, lane, own, SLC_TOP)
        sel = jnp.where(lane == own, 1.0, sel)
        selk_ref[...] = _dot(sel.astype(BF16), exp_ref[...])

        def slc_body(j, carry):
            k0 = pl.multiple_of(j * tq_n, tq_n)
            kt = ksw_ref[pl.ds(k0, tq_n), kcol].astype(BF16)
            vt = vsw_ref[pl.ds(k0, tq_n), kcol].astype(BF16)
            pos = k0 + lane
            s = _nt_dot(qg, kt) - slope * (tq - pos).astype(F32)
            mk = selk_ref[:, pl.ds(k0, tq_n)]
            mk = jnp.concatenate([mk] * r_n, axis=0) > 0.5
            return _softmax_update(carry, s, mk & (pos <= tq), vt)

        _, l_s, acc_s = lax.fori_loop(0, i + 1, slc_body, _softmax_init(rows, d))
        o_s = _safe_div(acc_s, l_s)

        carry = _softmax_init(rows, d)
        for jj in range(WINDOW // tq_n + 1):
            tile = i - WINDOW // tq_n + jj
            k0 = pl.multiple_of(jnp.maximum(tile, 0) * tq_n, tq_n)
            kt = ksw_ref[pl.ds(k0, tq_n), wcol].astype(BF16)
            vt = vsw_ref[pl.ds(k0, tq_n), wcol].astype(BF16)
            pos = tile * tq_n + lane
            dist = tq - pos
            s = _nt_dot(qg, kt) - slope * dist.astype(F32)
            carry = _softmax_update(carry, s, (pos >= 0) & (dist >= 0) & (dist <= WINDOW), vt)
        o_w = _safe_div(carry[2], carry[1])

        for r in range(r_n):
            h = g * r_n + r
            rs = slice(r * tq_n, (r + 1) * tq_n)
            o_ref[:, h * d:(h + 1) * d] = (gates[:, h:h + 1] * o_c[rs]
                                           + gates[:, NSA_HEADS + h:NSA_HEADS + h + 1] * o_s[rs]
                                           + gates[:, 2 * NSA_HEADS + h:2 * NSA_HEADS + h + 1] * o_w[rs])


def _nsa_prompt(p_even, kc, vc, b, t):
    tq_n = LANES
    nq = t // tq_n
    cov = _cover_matrix()
    exp = _expand_matrix(SLC_BLOCK, t)
    return pl.pallas_call(
        _nsa_kernel,
        grid=(b, nq),
        in_specs=[
            pl.BlockSpec((tq_n, NSA_HEADS * HEAD_DIM), lambda bi, i: (bi * nq + i, E_QN // 512)),
            pl.BlockSpec((tq_n, 2 * LANES), lambda bi, i: (bi * nq + i, E_GT // 256)),
            pl.BlockSpec((t, 2 * LANES), lambda bi, i: (bi, E_KS // 256)),
            pl.BlockSpec((t, 2 * LANES), lambda bi, i: (bi, E_VS // 256)),
            pl.BlockSpec((1, LANES, LANES), lambda bi, i: (bi, 0, 0)),
            pl.BlockSpec((1, LANES, LANES), lambda bi, i: (bi, 0, 0)),
            pl.BlockSpec((LANES, LANES), lambda bi, i: (0, 0)),
            pl.BlockSpec((LANES, t), lambda bi, i: (0, 0)),
        ],
        out_specs=pl.BlockSpec((tq_n, NSA_HEADS * HEAD_DIM), lambda bi, i: (bi * nq + i, 0)),
        out_shape=jax.ShapeDtypeStruct((b * t, NSA_HEADS * HEAD_DIM), F32),
        scratch_shapes=[pltpu.VMEM((tq_n, t), F32)],
        compiler_params=_params("parallel", "arbitrary"),
        name="nsa_prompt",
    )(p_even, p_even, p_even, p_even, kc, vc, cov, exp)


def _moba_kernel(q_ref, k_ref, v_ref, o_ref, km_ref):
    hp = pl.program_id(1)
    i = pl.program_id(2)
    tq_n = q_ref.shape[0]
    t = k_ref.shape[0]
    n_b = t // MOBA_BLOCK
    d = HEAD_DIM

    @pl.when(i == 0)
    def _():
        km_ref[...] = jnp.zeros_like(km_ref)
        for n in range(n_b):
            km_ref[n:n + 1, :] = jnp.mean(k_ref[n * MOBA_BLOCK:(n + 1) * MOBA_BLOCK, :], axis=0, keepdims=True)

    t0 = i * tq_n
    tq = t0 + lax.broadcasted_iota(jnp.int32, (tq_n, 1), 0)
    lane = lax.broadcasted_iota(jnp.int32, (1, LANES), 1)
    kpos = lax.broadcasted_iota(jnp.int32, (1, MOBA_BLOCK), 1)
    own = i

    for hh in range(2):
        col = slice(hh * d, (hh + 1) * d)
        hvec = jnp.zeros((1, 1), jnp.int32) + (hp * 2 + hh + 1)
        slope = jnp.exp2(-hvec.astype(F32))
        q = q_ref[:, col]
        route = _nt_dot(q, km_ref[:, col], precision=HIGHEST)
        cur = jnp.where(lane < own, route, NEG_INF)
        cur = jnp.where(lane < n_b, cur, -2e38)
        sel = _top_select(cur, lane, own, MOBA_TOP)
        sel = jnp.where(lane == own, 1.0, sel)
        qb = (q * QK_SCALE).astype(BF16)

        def body(n, carry):
            k0 = pl.multiple_of(n * MOBA_BLOCK, MOBA_BLOCK)
            kt = k_ref[pl.ds(k0, MOBA_BLOCK), col].astype(BF16)
            vt = v_ref[pl.ds(k0, MOBA_BLOCK), col].astype(BF16)
            pos = k0 + kpos
            s = _nt_dot(qb, kt) - slope * (tq - pos).astype(F32)
            chosen = jnp.sum(jnp.where(lane == n, sel, 0.0), axis=-1, keepdims=True) > 0.5
            return _softmax_update(carry, s, chosen & (pos <= tq), vt)

        _, l, acc = lax.fori_loop(0, i + 1, body, _softmax_init(tq_n, d))
        o_ref[:, col] = _safe_div(acc, l)


def _moba_prompt(p_even, b, t):
    tq_n = MOBA_BLOCK
    nq = t // tq_n
    return pl.pallas_call(
        _moba_kernel,
        grid=(b, MOBA_HEADS // 2, nq),
        in_specs=[
            pl.BlockSpec((tq_n, LANES), lambda bi, hp, i: (bi * nq + i, E_QM // LANES + hp)),
            pl.BlockSpec((t, LANES), lambda bi, hp, i: (bi, E_KM // LANES + hp)),
            pl.BlockSpec((t, LANES), lambda bi, hp, i: (bi, E_VM // LANES + hp)),
        ],
        out_specs=pl.BlockSpec((tq_n, LANES), lambda bi, hp, i: (bi * nq + i, hp)),
        out_shape=jax.ShapeDtypeStruct((b * t, MOBA_HEADS * HEAD_DIM), F32),
        scratch_shapes=[pltpu.VMEM((LANES, LANES), F32)],
        compiler_params=_params("parallel", "parallel", "arbitrary"),
        name="moba_prompt",
    )(p_even, p_even, p_even)


def _diff_lambda(lam_ref, lambda_init):
    lv = lam_ref[...]
    a = jnp.sum(lv[0:1] * lv[1:2], axis=-1, keepdims=True)
    c = jnp.sum(lv[2:3] * lv[3:4], axis=-1, keepdims=True)
    return jnp.exp(a) - jnp.exp(c) + lambda_init


def _diff_kernel(lambda_init, q_ref, k_ref, v_ref, lam_ref, sg_ref, o_ref):
    h = pl.program_id(1)
    i = pl.program_id(2)
    tq_n = q_ref.shape[0]
    d2 = 2 * HEAD_DIM
    t0 = i * tq_n
    row = lax.broadcasted_iota(jnp.int32, (2 * tq_n, 1), 0)
    tq = t0 + row % tq_n
    lane = lax.broadcasted_iota(jnp.int32, (1, d2), 1)
    kpos = lax.broadcasted_iota(jnp.int32, (1, tq_n), 1)
    slope = jnp.exp2(-(jnp.zeros((1, 1), jnp.int32) + (h + 1)).astype(F32))
    q = q_ref[...] * QK_SCALE
    q2 = jnp.concatenate([jnp.where(lane < HEAD_DIM, q, 0.0), jnp.where(lane >= HEAD_DIM, q, 0.0)], axis=0)
    q2 = q2.astype(BF16)

    def body(j, carry):
        k0 = pl.multiple_of(j * tq_n, tq_n)
        kt = k_ref[pl.ds(k0, tq_n), :].astype(BF16)
        vt = v_ref[pl.ds(k0, tq_n), :].astype(BF16)
        pos = k0 + kpos
        s = _nt_dot(q2, kt) - slope * (tq - pos).astype(F32)
        return _softmax_update(carry, s, pos <= tq, vt)

    _, l, acc = lax.fori_loop(0, i + 1, body, _softmax_init(2 * tq_n, d2))
    o = acc / l
    lam = _diff_lambda(lam_ref, lambda_init)
    attn = o[:tq_n] - lam * o[tq_n:]
    o_ref[...] = _rms(attn, sg_ref[...]) * (1.0 - lambda_init)


def _diff_prompt(p_odd, lam_rows, subln_gain, lambda_init, b, t):
    tq_n = 256
    nq = t // tq_n
    d2 = 2 * HEAD_DIM
    return pl.pallas_call(
        functools.partial(_diff_kernel, lambda_init),
        grid=(b, DIFF_HEADS, nq),
        in_specs=[
            pl.BlockSpec((tq_n, d2), lambda bi, h, i: (bi * nq + i, O_Q // d2 + h)),
            pl.BlockSpec((t, d2), lambda bi, h, i: (bi, O_K // d2 + h)),
            pl.BlockSpec((t, d2), lambda bi, h, i: (bi, O_V // d2 + h)),
            pl.BlockSpec(lam_rows.shape, lambda bi, h, i: (0, 0)),
            pl.BlockSpec((1, d2), lambda bi, h, i: (0, 0)),
        ],
        out_specs=pl.BlockSpec((tq_n, d2), lambda bi, h, i: (bi * nq + i, h)),
        out_shape=jax.ShapeDtypeStruct((b * t, DIFF_HEADS * d2), F32),
        compiler_params=_params("parallel", "parallel", "arbitrary"),
        name="diff_prompt",
    )(p_odd, p_odd, p_odd, lam_rows, subln_gain.reshape(1, d2))


def _tflash_init(dv, n):
    return (jnp.full((1, n), NEG_INF, F32), jnp.zeros((1, n), F32), jnp.zeros((dv, n), F32))


def _tflash_step(state, kt, qt, bias, vts, mask=None):
    m, l, acc = state
    st = _dot(kt, qt) + bias
    if mask is not None:
        st = jnp.where(mask, st, NEG_INF)
    m_new = jnp.maximum(m, jnp.max(st, axis=0, keepdims=True))
    alpha = jnp.exp(m - m_new)
    p = jnp.exp(st - m_new)
    l = alpha * l + jnp.sum(p, axis=0, keepdims=True)
    pb = p.astype(BF16)
    seg = pb.shape[1] // len(vts)
    pv = [_dot(vt, pb[:, s * seg:(s + 1) * seg]) for s, vt in enumerate(vts)]
    acc = alpha * acc + (pv[0] if len(pv) == 1 else jnp.concatenate(pv, axis=1))
    return m_new, l, acc


def _top_select_rows(cur, row, own, n_top):
    sel = jnp.zeros(cur.shape, F32)
    for _ in range(n_top):
        mx = jnp.max(cur, axis=0, keepdims=True)
        idx = jnp.min(jnp.where(cur == mx, row, 4 * LANES), axis=0, keepdims=True)
        pick = row == idx
        sel = jnp.where(pick, jnp.where(row < own, 1.0, 0.0), sel)
        cur = jnp.where(pick, -3e38, cur)
    return sel


def _transpose_into(dst_ref, src_ref, col0):
    for n in range(src_ref.shape[0] // LANES):
        dst_ref[:, n * LANES:(n + 1) * LANES] = src_ref[n * LANES:(n + 1) * LANES, col0:col0 + LANES].T


def _lane_tile(x, n):
    return x if n == 1 else jnp.concatenate([x] * n, axis=1)


def _pos_lanes(t):
    return jnp.asarray(np.broadcast_to(np.arange(t, dtype=np.float32)[:, None], (t, LANES)))


def _nsa_attn_kernel(q_ref, gt_ref, ksw_ref, vsw_ref, kc_ref, vc_ref, covt_ref, expt_ref, pos_ref, o_ref,
                     vst_ref, vwt_ref, vct_ref, selk_ref):
    i = pl.program_id(1)
    nq = q_ref.shape[0]
    d, r_n = HEAD_DIM, NSA_GROUP
    n_cb = NSA_HEADS * d // LANES

    @pl.when(i == 0)
    def _():
        _transpose_into(vst_ref, vsw_ref, 0)
        _transpose_into(vwt_ref, vsw_ref, LANES)
        vct_ref[...] = vc_ref[0].T

    t0 = i * nq
    lane = lax.broadcasted_iota(jnp.int32, (1, nq), 1)
    row = lax.broadcasted_iota(jnp.int32, (LANES, 1), 0)
    tq = t0 + lane
    own = tq // SLC_BLOCK
    tk = 4 * LANES
    krow = lax.broadcasted_iota(jnp.int32, (tk, 1), 0)
    gates_t = jax.nn.sigmoid(gt_ref[:, :LANES]).T
    q_t = [q_ref[:, cb * LANES:(cb + 1) * LANES].T for cb in range(n_cb)]
    c_end = _lane_tile(pos_ref[0:LANES, :], nq // LANES) * CMP_STRIDE + (CMP_BLOCK - 1)
    c_valid = c_end <= tq.astype(F32)
    head_out = [None] * NSA_HEADS

    for g in range(NSA_KV_HEADS):
        g_rows = row // d == g
        heads = [g * r_n + r for r in range(r_n)]
        slopes = [2.0 ** -(h + 1) for h in heads]

        def head_qt(h):
            x = q_t[h // 2]
            if h % 2 != g:
                x = pltpu.roll(x, d, 0)
            return (jnp.where(g_rows, x, 0.0) * QK_SCALE).astype(BF16)

        qt = jnp.concatenate([head_qt(h) for h in heads], axis=1)
        n_all = r_n * nq

        def head_bias(base):
            base = _lane_tile(base, nq // base.shape[1])
            return jnp.concatenate([s * base for s in slopes], axis=1)

        vrows = slice(g * d, (g + 1) * d)

        cm = _lane_tile(c_valid, r_n)
        s = jnp.where(cm, _dot(kc_ref[0].astype(BF16), qt) + head_bias(c_end), NEG_INF)
        p = jnp.where(cm, jnp.exp(s - jnp.max(s, axis=0, keepdims=True)), 0.0)
        l = jnp.sum(p, axis=0, keepdims=True)
        p = p * jnp.where(l > 0.0, 1.0 / jnp.where(l > 0.0, l, 1.0), 0.0)
        o_c = _dot(vct_ref[vrows, :].astype(BF16), p.astype(BF16))
        p_sum = p[:, :nq]
        for r in range(1, r_n):
            p_sum = p_sum + p[:, r * nq:(r + 1) * nq]

        imp = _dot(covt_ref[...], p_sum, precision=HIGHEST)
        cur = jnp.where(row < own, imp, -1.0)
        sel = _top_select_rows(cur, row, own, SLC_TOP)
        sel = jnp.where(row == own, 1.0, sel)
        selk_ref[...] = _dot(expt_ref[...], sel.astype(BF16))

        def slc_tile(state, k0, diag):
            kt = ksw_ref[pl.ds(k0, tk), 0:LANES].astype(BF16)
            vt = vst_ref[vrows, pl.ds(k0, tk)].astype(BF16)
            mk = selk_ref[pl.ds(k0, tk), :] > 0.5
            if diag:
                mk = mk & (k0 + krow <= tq)
            return _tflash_step(state, kt, qt, head_bias(pos_ref[pl.ds(k0, tk), :]), [vt], _lane_tile(mk, r_n))

        j_diag = t0 // tk
        s_state = slc_tile(_tflash_init(d, n_all), pl.multiple_of(j_diag * tk, tk), True)
        _, l_s, acc_s = lax.fori_loop(0, j_diag, lambda j, st: slc_tile(st, pl.multiple_of(j * tk, tk), False),
                                      s_state)

        k0 = pl.multiple_of(jnp.maximum(t0 - WINDOW, 0), nq)
        wrow = k0 + lax.broadcasted_iota(jnp.int32, (WINDOW + nq, 1), 0)
        dist = tq - wrow
        kt = ksw_ref[pl.ds(k0, WINDOW + nq), LANES:2 * LANES].astype(BF16)
        vt = vwt_ref[vrows, pl.ds(k0, WINDOW + nq)].astype(BF16)
        wmask = _lane_tile((dist >= 0) & (dist <= WINDOW), r_n)
        _, l_w, acc_w = _tflash_step(_tflash_init(d, n_all), kt, qt, head_bias(pos_ref[pl.ds(k0, WINDOW + nq), :]),
                                     [vt], wmask)

        o_s = acc_s / l_s
        o_w = acc_w / l_w
        for r, h in enumerate(heads):
            seg = slice(r * nq, (r + 1) * nq)
            head_out[h] = (gates_t[h:h + 1] * o_c[:, seg]
                           + gates_t[NSA_HEADS + h:NSA_HEADS + h + 1] * o_s[:, seg]
                           + gates_t[2 * NSA_HEADS + h:2 * NSA_HEADS + h + 1] * o_w[:, seg])

    for cb in range(n_cb):
        o_ref[:, cb * LANES:(cb + 1) * LANES] = jnp.concatenate([head_out[2 * cb], head_out[2 * cb + 1]], axis=0).T


def _nsa_attn(p_even, kc, vc, b, t):
    nq = 2 * LANES
    n_tiles = t // nq
    covt = _cover_matrix().T
    expt = _expand_matrix(SLC_BLOCK, t).T
    pos = _pos_lanes(t)
    return pl.pallas_call(
        _nsa_attn_kernel,
        grid=(b, n_tiles),
        in_specs=[
            pl.BlockSpec((nq, NSA_HEADS * HEAD_DIM), lambda bi, i: (bi * n_tiles + i, E_QN // 512)),
            pl.BlockSpec((nq, 2 * LANES), lambda bi, i: (bi * n_tiles + i, E_GT // 256)),
            pl.BlockSpec((t, 2 * LANES), lambda bi, i: (bi, E_KS // 256)),
            pl.BlockSpec((t, 2 * LANES), lambda bi, i: (bi, E_VS // 256)),
            pl.BlockSpec((1, LANES, LANES), lambda bi, i: (bi, 0, 0)),
            pl.BlockSpec((1, LANES, LANES), lambda bi, i: (bi, 0, 0)),
            pl.BlockSpec((LANES, LANES), lambda bi, i: (0, 0)),
            pl.BlockSpec((t, LANES), lambda bi, i: (0, 0)),
            pl.BlockSpec((t, LANES), lambda bi, i: (0, 0)),
        ],
        out_specs=pl.BlockSpec((nq, NSA_HEADS * HEAD_DIM), lambda bi, i: (bi * n_tiles + i, 0)),
        out_shape=jax.ShapeDtypeStruct((b * t, NSA_HEADS * HEAD_DIM), F32),
        scratch_shapes=[pltpu.VMEM((LANES, t), F32), pltpu.VMEM((LANES, t), F32), pltpu.VMEM((LANES, LANES), F32),
                        pltpu.VMEM((t, nq), F32)],
        compiler_params=_params("parallel", "arbitrary"),
        name="nsa_attn",
    )(p_even, p_even, p_even, p_even, kc, vc, covt, expt, pos)


def _moba_attn_kernel(q_ref, k_ref, v_ref, pos_ref, o_ref, km_ref, vt_ref):
    hp = pl.program_id(1)
    i = pl.program_id(2)
    nq = q_ref.shape[0]
    n_b = k_ref.shape[0] // MOBA_BLOCK
    d = HEAD_DIM

    @pl.when(i == 0)
    def _():
        for n in range(n_b):
            km_ref[n:n + 1, :] = jnp.mean(k_ref[n * MOBA_BLOCK:(n + 1) * MOBA_BLOCK, :], axis=0, keepdims=True)
        _transpose_into(vt_ref, v_ref, 0)

    bpt = 2
    tk = bpt * MOBA_BLOCK
    lane = lax.broadcasted_iota(jnp.int32, (1, nq), 1)
    row = lax.broadcasted_iota(jnp.int32, (LANES, 1), 0)
    krow = lax.broadcasted_iota(jnp.int32, (tk, 1), 0)
    brow = lax.broadcasted_iota(jnp.int32, (n_b, 1), 0)
    tq = i * nq + lane
    q_t = q_ref[...].T
    qts, sels, biases = [], [], []
    for hh in range(2):
        qh = jnp.where(row // d == hh, q_t, 0.0)
        route = _dot(km_ref[...], qh, precision=HIGHEST)
        cur = jnp.where(brow < i, route, NEG_INF)
        sel = _top_select_rows(cur, brow, i, MOBA_TOP)
        sels.append(jnp.where(brow == i, 1.0, sel))
        qts.append((qh * QK_SCALE).astype(BF16))
        biases.append(jnp.exp2(-(jnp.zeros((1, 1), jnp.int32) + (hp * 2 + hh + 1)).astype(F32)))
    qt = jnp.concatenate(qts, axis=1)

    def tile(state, j, diag):
        k0 = pl.multiple_of(j * tk, tk)
        kt = k_ref[pl.ds(k0, tk), :].astype(BF16)
        base = pos_ref[pl.ds(k0, tk), :]
        vts = [vt_ref[hh * d:(hh + 1) * d, pl.ds(k0, tk)].astype(BF16) for hh in range(2)]
        bias = jnp.concatenate([_lane_tile(biases[hh] * base, nq // LANES) for hh in range(2)], axis=1)
        masks = []
        for hh in range(2):
            chosen = [jnp.max(jnp.where(brow == j * bpt + bb, sels[hh], 0.0), axis=0, keepdims=True)
                      for bb in range(bpt)]
            mk = jnp.where(krow < MOBA_BLOCK, chosen[0], chosen[1]) > 0.5
            if diag:
                mk = mk & (k0 + krow <= tq)
            masks.append(mk)
        return _tflash_step(state, kt, qt, bias, vts, jnp.concatenate(masks, axis=1))

    j_diag = i // bpt
    state = tile(_tflash_init(d, 2 * nq), j_diag, True)
    _, l, acc = lax.fori_loop(0, j_diag, lambda j, st: tile(st, j, False), state)
    o = acc / l
    o_ref[...] = jnp.concatenate([o[:, :nq], o[:, nq:]], axis=0).T


def _moba_attn(p_even, b, t):
    nq = MOBA_BLOCK
    n_tiles = t // nq
    return pl.pallas_call(
        _moba_attn_kernel,
        grid=(b, MOBA_HEADS // 2, n_tiles),
        in_specs=[
            pl.BlockSpec((nq, LANES), lambda bi, hp, i: (bi * n_tiles + i, E_QM // LANES + hp)),
            pl.BlockSpec((t, LANES), lambda bi, hp, i: (bi, E_KM // LANES + hp)),
            pl.BlockSpec((t, LANES), lambda bi, hp, i: (bi, E_VM // LANES + hp)),
            pl.BlockSpec((t, LANES), lambda bi, hp, i: (0, 0)),
        ],
        out_specs=pl.BlockSpec((nq, LANES), lambda bi, hp, i: (bi * n_tiles + i, hp)),
        out_shape=jax.ShapeDtypeStruct((b * t, MOBA_HEADS * HEAD_DIM), F32),
        scratch_shapes=[pltpu.VMEM((t // MOBA_BLOCK, LANES), F32), pltpu.VMEM((LANES, t), F32)],
        compiler_params=_params("parallel", "parallel", "arbitrary"),
        name="moba_attn",
    )(p_even, p_even, p_even, _pos_lanes(t))


def _diff_attn_kernel(lambda_init, q_ref, k_ref, v_ref, pos_ref, lam_ref, sg_ref, o_ref, vt_ref):
    h = pl.program_id(1)
    i = pl.program_id(2)
    nq = q_ref.shape[0]
    d = HEAD_DIM

    @pl.when(i == 0)
    def _():
        _transpose_into(vt_ref, v_ref, 0)

    tk = 4 * LANES
    lane = lax.broadcasted_iota(jnp.int32, (1, nq), 1)
    row = lax.broadcasted_iota(jnp.int32, (LANES, 1), 0)
    krow = lax.broadcasted_iota(jnp.int32, (tk, 1), 0)
    tq = i * nq + lane
    slope = jnp.exp2(-(jnp.zeros((1, 1), jnp.int32) + (h + 1)).astype(F32))
    q_t = q_ref[...].T * QK_SCALE
    qt = jnp.concatenate([jnp.where(row < d, q_t, 0.0), jnp.where(row >= d, q_t, 0.0)], axis=1).astype(BF16)

    def tile(state, j, diag):
        k0 = pl.multiple_of(j * tk, tk)
        kt = k_ref[pl.ds(k0, tk), :].astype(BF16)
        vt = vt_ref[:, pl.ds(k0, tk)].astype(BF16)
        bias = _lane_tile(slope * pos_ref[pl.ds(k0, tk), :], 2 * nq // LANES)
        mask = _lane_tile(k0 + krow <= tq, 2) if diag else None
        return _tflash_step(state, kt, qt, bias, [vt], mask)

    j_diag = i * nq // tk
    state = tile(_tflash_init(2 * d, 2 * nq), j_diag, True)
    _, l, acc = lax.fori_loop(0, j_diag, lambda j, st: tile(st, j, False), state)
    o = acc / l
    attn_t = o[:, :nq] - _diff_lambda(lam_ref, lambda_init) * o[:, nq:]
    o_ref[...] = _rms(attn_t.T, sg_ref[...]) * (1.0 - lambda_init)


def _diff_attn(p_odd, lam_rows, subln_gain, lambda_init, b, t):
    nq = 512
    n_tiles = t // nq
    d2 = 2 * HEAD_DIM
    return pl.pallas_call(
        functools.partial(_diff_attn_kernel, lambda_init),
        grid=(b, DIFF_HEADS, n_tiles),
        in_specs=[
            pl.BlockSpec((nq, d2), lambda bi, h, i: (bi * n_tiles + i, O_Q // d2 + h)),
            pl.BlockSpec((t, d2), lambda bi, h, i: (bi, O_K // d2 + h)),
            pl.BlockSpec((t, d2), lambda bi, h, i: (bi, O_V // d2 + h)),
            pl.BlockSpec((t, LANES), lambda bi, h, i: (0, 0)),
            pl.BlockSpec(lam_rows.shape, lambda bi, h, i: (0, 0)),
            pl.BlockSpec((1, d2), lambda bi, h, i: (0, 0)),
        ],
        out_specs=pl.BlockSpec((nq, d2), lambda bi, h, i: (bi * n_tiles + i, h)),
        out_shape=jax.ShapeDtypeStruct((b * t, DIFF_HEADS * d2), F32),
        scratch_shapes=[pltpu.VMEM((d2, t), F32)],
        compiler_params=_params("parallel", "parallel", "arbitrary"),
        name="diff_attn",
    )(p_odd, p_odd, p_odd, _pos_lanes(t), lam_rows, subln_gain.reshape(1, d2))


def _decode_softmax(s, mask, s_new):
    sm = jnp.where(mask, s, NEG_INF)
    m = jnp.maximum(jnp.max(sm, axis=-1, keepdims=True), s_new)
    p = jnp.where(mask, jnp.exp(sm - m), 0.0)
    p_new = jnp.exp(s_new - m)
    l = jnp.sum(p, axis=-1, keepdims=True) + p_new
    return p, p_new, l


def _nsa_decode_kernel(n_pages, pt_ref, *refs):
    ks_pages = refs[:n_pages]
    vs_pages = refs[n_pages:2 * n_pages]
    (q_ref, gt_ref, kc_ref, vc_ref, wk_ref, wv_ref, new_ref, cov_ref, exp_ref, o_ref) = refs[2 * n_pages:]
    d = HEAD_DIM
    t = n_pages * PAGE_SIZE
    hrow = lax.broadcasted_iota(jnp.int32, (NSA_HEADS, 1), 0)
    grp0 = hrow < NSA_GROUP
    slope = jnp.exp2(-(hrow + 1).astype(F32))
    lane = lax.broadcasted_iota(jnp.int32, (1, LANES), 1)
    q64 = q_ref[0] * QK_SCALE
    q = jnp.where((lane >= d) == (hrow >= NSA_GROUP), jnp.concatenate([q64, q64], axis=1), 0.0)
    qb = q.astype(BF16)
    gates = jax.nn.sigmoid(gt_ref[0])
    new = new_ref[0]

    def own_group(o):
        return jnp.where(grp0, o[:, :d], o[:, d:])

    def new_row(r):
        return own_group(new[r:r + 1, :])

    c_end = lane * CMP_STRIDE + (CMP_BLOCK - 1)
    s = _nt_dot(qb, kc_ref[0].astype(BF16)) - slope * (t - c_end).astype(F32)
    valid = c_end <= t
    sm = jnp.where(valid, s, NEG_INF)
    m = jnp.max(sm, axis=-1, keepdims=True)
    p = jnp.where(valid, jnp.exp(sm - m), 0.0)
    p_c = p / jnp.sum(p, axis=-1, keepdims=True)
    o_c = own_group(_dot(p_c.astype(BF16), vc_ref[0].astype(BF16)))

    ps0 = jnp.sum(jnp.where(grp0, p_c, 0.0), axis=0, keepdims=True)
    ps1 = jnp.sum(jnp.where(grp0, 0.0, p_c), axis=0, keepdims=True)
    p_sum = jnp.where(grp0, ps0, ps1)
    imp = _dot(p_sum, cov_ref[...], precision=HIGHEST)
    own = t // SLC_BLOCK
    cur = jnp.where(lane < own, imp, -1.0)
    sel = _top_select(cur, lane, own, SLC_TOP)
    selk = _dot(sel.astype(BF16), exp_ref[...]) > 0.5

    ks = jnp.concatenate([pg[0] for pg in ks_pages], axis=1).astype(BF16)
    vs = jnp.concatenate([pg[0] for pg in vs_pages], axis=1).astype(BF16)
    pos = lax.broadcasted_iota(jnp.int32, (1, t), 1)
    s = _dot(qb, ks) - slope * (t - pos).astype(F32)
    s_new = jnp.sum(q * new[0:1, :], axis=-1, keepdims=True)
    p, p_new, l = _decode_softmax(s, selk, s_new)
    o_s = (own_group(_nt_dot(p.astype(BF16), vs)) + p_new * new_row(1)) / l

    wk = wk_ref[0].astype(BF16)
    wv = wv_ref[0].astype(BF16)
    w_n = wk.shape[1]
    wpos = (t - w_n) + lax.broadcasted_iota(jnp.int32, (1, w_n), 1)
    s = _dot(qb, wk) - slope * (t - wpos).astype(F32)
    s_new = jnp.sum(q * new[2:3, :], axis=-1, keepdims=True)
    p, p_new, l = _decode_softmax(s, (t - wpos) <= WINDOW, s_new)
    o_w = (own_group(_nt_dot(p.astype(BF16), wv)) + p_new * new_row(3)) / l

    o_ref[0] = gates[:, 0:1] * o_c + gates[:, 1:2] * o_s + gates[:, 2:3] * o_w


def _nsa_decode(page_table, ks_pool, vs_pool, q8, gates8, kc, vc, win_k, win_v, new_rows):
    b, n_pages = page_table.shape
    t = n_pages * PAGE_SIZE
    gd = NSA_KV_HEADS * HEAD_DIM
    page_specs = [pl.BlockSpec((1, gd, PAGE_SIZE), lambda i, pt, p=p: (pt[i * n_pages + p], 0, 0))
                  for p in range(n_pages)]
    per_b = lambda a: pl.BlockSpec((1,) + a.shape[1:], lambda i, pt: (i,) + (0,) * (a.ndim - 1))
    const = lambda a: pl.BlockSpec(a.shape, lambda i, pt: (0,) * a.ndim)
    cov = _cover_matrix()
    exp = _expand_matrix(SLC_BLOCK, t)
    per_sample = [q8, gates8, kc, vc, win_k, win_v, new_rows]
    return pl.pallas_call(
        functools.partial(_nsa_decode_kernel, n_pages),
        grid_spec=pltpu.PrefetchScalarGridSpec(
            num_scalar_prefetch=1,
            grid=(b,),
            in_specs=page_specs + page_specs + [per_b(a) for a in per_sample] + [const(cov), const(exp)],
            out_specs=pl.BlockSpec((1, NSA_HEADS, HEAD_DIM), lambda i, pt: (i, 0, 0)),
        ),
        out_shape=jax.ShapeDtypeStruct((b, NSA_HEADS, HEAD_DIM), F32),
        compiler_params=_params("parallel"),
        name="nsa_decode",
    )(page_table.reshape(-1), *([ks_pool] * n_pages), *([vs_pool] * n_pages), *per_sample, cov, exp)


def _moba_decode_kernel(n_pages, pt_ref, *refs):
    k_pages = refs[:n_pages]
    v_pages = refs[n_pages:2 * n_pages]
    q_ref, new_ref, exp_ref, o_ref = refs[2 * n_pages:]
    d = HEAD_DIM
    hd = MOBA_HEADS * d
    t = n_pages * PAGE_SIZE
    ppb = MOBA_BLOCK // PAGE_SIZE
    n_b = n_pages // ppb
    hrow = lax.broadcasted_iota(jnp.int32, (MOBA_HEADS, 1), 0)
    slope = jnp.exp2(-(hrow + 1).astype(F32))
    lane = lax.broadcasted_iota(jnp.int32, (1, LANES), 1)
    head_of_lane = lax.broadcasted_iota(jnp.int32, (1, hd), 1) // d
    own_head = head_of_lane == hrow
    q = jnp.where(own_head, q_ref[0], 0.0)
    new = new_ref[0]

    k_mean = jnp.zeros((hd, LANES), F32)
    for n in range(n_b):
        blk = k_pages[n * ppb][0]
        for pp in range(1, ppb):
            blk = blk + k_pages[n * ppb + pp][0]
        col = jnp.sum(blk, axis=-1, keepdims=True) * (1.0 / MOBA_BLOCK)
        k_mean = jnp.where(lane == n, col, k_mean)
    route = _dot(q, k_mean, precision=HIGHEST)
    cur = jnp.where(lane < n_b, route, -2e38)
    sel = _top_select(cur, lane, n_b, MOBA_TOP)
    selk = _dot(sel.astype(BF16), exp_ref[...]) > 0.5

    k_all = jnp.concatenate([pg[0] for pg in k_pages], axis=1).astype(BF16)
    v_all = jnp.concatenate([pg[0] for pg in v_pages], axis=1).astype(BF16)
    qs = q * QK_SCALE
    pos = lax.broadcasted_iota(jnp.int32, (1, t), 1)
    s = _dot(qs.astype(BF16), k_all) - slope * (t - pos).astype(F32)
    s_new = jnp.sum(qs * new[0:1], axis=-1, keepdims=True)
    p, p_new, l = _decode_softmax(s, selk, s_new)
    o = (_nt_dot(p.astype(BF16), v_all) + p_new * new[1:2]) / l
    o_ref[0] = jnp.sum(jnp.where(own_head, o, 0.0), axis=0, keepdims=True)


def _moba_decode(page_table, k_pool, v_pool, q_rows, new_rows):
    b, n_pages = page_table.shape
    t = n_pages * PAGE_SIZE
    hd = MOBA_HEADS * HEAD_DIM
    page_specs = [pl.BlockSpec((1, hd, PAGE_SIZE), lambda i, pt, p=p: (pt[i * n_pages + p], 0, 0))
                  for p in range(n_pages)]
    per_b = lambda a: pl.BlockSpec((1,) + a.shape[1:], lambda i, pt: (i,) + (0,) * (a.ndim - 1))
    exp = _expand_matrix(MOBA_BLOCK, t)
    return pl.pallas_call(
        functools.partial(_moba_decode_kernel, n_pages),
        grid_spec=pltpu.PrefetchScalarGridSpec(
            num_scalar_prefetch=1,
            grid=(b,),
            in_specs=page_specs + page_specs + [per_b(q_rows), per_b(new_rows),
                                                pl.BlockSpec(exp.shape, lambda i, pt: (0, 0))],
            out_specs=pl.BlockSpec((1, 1, hd), lambda i, pt: (i, 0, 0)),
        ),
        out_shape=jax.ShapeDtypeStruct((b, 1, hd), F32),
        compiler_params=_params("parallel"),
        name="moba_decode",
    )(page_table.reshape(-1), *([k_pool] * n_pages), *([v_pool] * n_pages), q_rows, new_rows, exp)


def _diff_decode_kernel(lambda_init, n_step, n_total, pt_ref, *refs):
    k_pages = refs[:n_step]
    v_pages = refs[n_step:2 * n_step]
    q_ref, knew_ref, vnew_ref, lam_ref, sg_ref, o_ref, m_ref, l_ref, acc_ref = refs[2 * n_step:]
    step = pl.program_id(1)
    d = HEAD_DIM
    nh = DIFF_HEADS
    t = n_total * PAGE_SIZE
    row = lax.broadcasted_iota(jnp.int32, (2 * nh, 1), 0)
    lane = lax.broadcasted_iota(jnp.int32, (1, 2 * d), 1)
    slope = jnp.exp2(-(row % nh + 1).astype(F32))
    q8 = q_ref[0] * QK_SCALE
    q = jnp.concatenate([jnp.where(lane < d, q8, 0.0), jnp.where(lane >= d, q8, 0.0)], axis=0)

    @pl.when(step == 0)
    def _():
        m_ref[...] = jnp.full(m_ref.shape, NEG_INF, F32)
        l_ref[...] = jnp.zeros_like(l_ref)
        acc_ref[...] = jnp.zeros_like(acc_ref)

    k_all = jnp.concatenate([pg[0].reshape(PAGE_SIZE * nh, 2 * d) for pg in k_pages], axis=0).astype(BF16)
    v_all = jnp.concatenate([pg[0].reshape(PAGE_SIZE * nh, 2 * d) for pg in v_pages], axis=0).astype(BF16)
    n_k = n_step * PAGE_SIZE * nh
    col = lax.broadcasted_iota(jnp.int32, (1, n_k), 1)
    pos = step * (n_step * PAGE_SIZE) + col // nh
    mask = col % nh == row % nh
    s = _nt_dot(q.astype(BF16), k_all) - slope * (t - pos).astype(F32)
    s = jnp.where(mask, s, NEG_INF)
    m_old = m_ref[:, 0:1]
    m_new = jnp.maximum(m_old, jnp.max(s, axis=-1, keepdims=True))
    alpha = jnp.exp(m_old - m_new)
    p = jnp.where(mask, jnp.exp(s - m_new), 0.0)
    l_new = alpha * l_ref[:, 0:1] + jnp.sum(p, axis=-1, keepdims=True)
    acc_new = alpha * acc_ref[...] + _dot(p.astype(BF16), v_all)
    m_ref[...] = jnp.broadcast_to(m_new, m_ref.shape)
    l_ref[...] = jnp.broadcast_to(l_new, l_ref.shape)
    acc_ref[...] = acc_new

    @pl.when(step == pl.num_programs(1) - 1)
    def _():
        k_new = jnp.concatenate([knew_ref[0]] * 2, axis=0)
        v_new = jnp.concatenate([vnew_ref[0]] * 2, axis=0)
        s_new = jnp.sum(q * k_new, axis=-1, keepdims=True)
        m_fin = jnp.maximum(m_new, s_new)
        a = jnp.exp(m_new - m_fin)
        p_new = jnp.exp(s_new - m_fin)
        o = (a * acc_new + p_new * v_new) / (a * l_new + p_new)
        lam = _diff_lambda(lam_ref, lambda_init)
        attn = o[:nh] - lam * o[nh:]
        o_ref[0] = _rms(attn, sg_ref[...]) * (1.0 - lambda_init)


def _diff_decode(page_table, k_pool, v_pool, q_rows, k_new, v_new, lam_rows, subln_gain, lambda_init):
    b, n_pages = page_table.shape
    nh, d2 = DIFF_HEADS, 2 * HEAD_DIM
    n_step = n_pages // 2
    page_specs = [pl.BlockSpec((1, PAGE_SIZE, nh, d2),
                               lambda i, s, pt, p=p: (pt[i * n_pages + s * n_step + p], 0, 0, 0))
                  for p in range(n_step)]
    per_b = pl.BlockSpec((1, nh, d2), lambda i, s, pt: (i, 0, 0))
    const = lambda a: pl.BlockSpec(a.shape, lambda i, s, pt: (0,) * a.ndim)
    sg = subln_gain.reshape(1, d2)
    return pl.pallas_call(
        functools.partial(_diff_decode_kernel, lambda_init, n_step, n_pages),
        grid_spec=pltpu.PrefetchScalarGridSpec(
            num_scalar_prefetch=1,
            grid=(b, n_pages // n_step),
            in_specs=page_specs + page_specs + [per_b, per_b, per_b, const(lam_rows), const(sg)],
            out_specs=pl.BlockSpec((1, nh, d2), lambda i, s, pt: (i, 0, 0)),
            scratch_shapes=[pltpu.VMEM((2 * nh, LANES), F32), pltpu.VMEM((2 * nh, LANES), F32),
                            pltpu.VMEM((2 * nh, d2), F32)],
        ),
        out_shape=jax.ShapeDtypeStruct((b, nh, d2), F32),
        compiler_params=_params("parallel", "arbitrary"),
        name="diff_decode",
    )(page_table.reshape(-1), *([k_pool] * n_step), *([v_pool] * n_step), q_rows, k_new, v_new, lam_rows, sg)


def _even_weights(w_in, w_out, q_gain, ks_gain, kw_gain, mq_gain, mk_gain):
    d, g_n, ha, hb = HEAD_DIM, NSA_KV_HEADS, NSA_HEADS, MOBA_HEADS
    sizes = [ha * d] + [g_n * d] * 6 + [3 * ha, hb * d, hb * d, hb * d]
    offs = np.concatenate([[0], np.cumsum(sizes)])
    q_n, kc, vc, ks, vs, kw, vw, gates, q_m, k_m, v_m = [w_in[:, offs[i]:offs[i + 1]] for i in range(11)]
    gates = gates.reshape(-1, ha, 3).transpose(0, 2, 1).reshape(-1, 3 * ha)
    pad = jnp.zeros((w_in.shape[0], E_COLS - E_GT - 3 * ha), w_in.dtype)
    w = jnp.concatenate([q_n, ks, kw, q_m, k_m, kc, vc, vs, vw, v_m, gates, pad], axis=1)
    ones = jnp.ones((E_COLS - E_KC,), F32)
    col_gain = jnp.concatenate([jnp.tile(q_gain, ha), jnp.tile(ks_gain, g_n), jnp.tile(kw_gain, g_n),
                                jnp.tile(mq_gain, hb), jnp.tile(mk_gain, hb), ones])
    w_out = w_out.astype(BF16)
    return w, col_gain, w_out[:ha * d], w_out[ha * d:]


def _odd_weights(w_in, w_out, q_gain, k_gain):
    n_half = 2 * DIFF_HEADS
    col_gain = jnp.concatenate([jnp.tile(q_gain, n_half), jnp.tile(k_gain, n_half),
                                jnp.ones((O_COLS - O_V,), F32)])
    return w_in.astype(BF16), col_gain, w_out.astype(BF16)


def _trunk_prompt(x, prm):
    b, t, dm = x.shape
    d, g_n = HEAD_DIM, NSA_KV_HEADS
    gd = g_n * d
    n_pages = t // PAGE_SIZE
    x = x.reshape(b * t, dm)
    x = _ffn(x, prm["ffn1_norm"][0], prm["ffn1_w_in"][0], prm["ffn1_w_out"][0])
    pe = _proj(x, prm["mix_norm"][0], prm["even_w"], prm["even_gain"], E_FLAGS)
    col = lambda off, width: pe[:, off:off + width]
    kc_rows, vc_rows = col(E_KC, gd), col(E_VC, gd)
    kc, vc = _compress_prompt(pe, prm["cmp_k"], prm["cmp_v"], prm["nsa_kc_gain"], b, t)
    o_nsa = _nsa_attn(pe, kc, vc, b, t)
    o_moba = _moba_attn(pe, b, t)
    x = _oproj(x, [o_nsa, o_moba], [prm["even_wo_nsa"], prm["even_wo_moba"]])
    x = _ffn(x, prm["ffn2_norm"][0], prm["ffn2_w_in"][0], prm["ffn2_w_out"][0])
    n_keep = min(WINDOW, t)
    last = lambda a: a.reshape(b, t, g_n, d)[:, t - n_keep:]
    even = (kc_rows.reshape(b, t, g_n, d), vc_rows.reshape(b, t, g_n, d),
            col(E_KS, gd).reshape(b, t, g_n, d), col(E_VS, gd).reshape(b, t, g_n, d),
            col(E_KM, MOBA_HEADS * d).reshape(b, t, MOBA_HEADS, d),
            col(E_VM, MOBA_HEADS * d).reshape(b, t, MOBA_HEADS, d),
            last(col(E_KW, gd)), last(col(E_VW, gd)))
    x = _ffn(x, prm["ffn1_norm"][1], prm["ffn1_w_in"][1], prm["ffn1_w_out"][1])
    po = _proj(x, prm["mix_norm"][1], prm["odd_w"], prm["odd_gain"], O_FLAGS)
    o_diff = _diff_attn(po, prm["lam_rows"], prm["subln_gain"], prm["lambda_init"], b, t)
    x = _oproj(x, [o_diff], [prm["odd_wo"]])
    x = _ffn(x, prm["ffn2_norm"][1], prm["ffn2_w_in"][1], prm["ffn2_w_out"][1])
    odd = (po[:, O_K:O_V].reshape(b, t, DIFF_HEADS, 2 * d), po[:, O_V:].reshape(b, t, DIFF_HEADS, 2 * d))
    return x.reshape(b, t, dm), even, odd


def _trunk_sample(x, prm, past):
    b, t, dm = x.shape
    d, g_n = HEAD_DIM, NSA_KV_HEADS
    gd = g_n * d
    hm = MOBA_HEADS * d
    pt = past["page_table"]
    x = x.reshape(b, dm)
    x = _ffn(x, prm["ffn1_norm"][0], prm["ffn1_w_in32"][0], prm["ffn1_w_out32"][0])
    pe = _proj(x, prm["mix_norm"][0], prm["even_w32"], prm["even_gain"], E_FLAGS)
    col = lambda off, width: pe[:, off:off + width]
    pool_t = lambda c: jnp.transpose(c, (0, 2, 3, 1)).reshape(c.shape[0], -1, c.shape[1])
    kc, vc = _compress_decode(pt, pool_t(past["cmp_k"]), pool_t(past["cmp_v"]), prm["cmp_k"], prm["cmp_v"],
                              prm["nsa_kc_gain"])
    q8 = col(E_QN, NSA_HEADS * d).reshape(b, NSA_HEADS, d)
    gates = col(E_GT, 3 * NSA_HEADS).reshape(b, 3, NSA_HEADS).transpose(0, 2, 1)
    gates8 = jnp.pad(gates, ((0, 0), (0, 0), (0, LANES - 3)))
    new_nsa = jnp.stack([col(E_KS, gd), col(E_VS, gd), col(E_KW, gd), col(E_VW, gd)], axis=1)
    new_nsa = jnp.pad(new_nsa, ((0, 0), (0, 4), (0, 0)))
    o_nsa = _nsa_decode(pt, pool_t(past["slc_k"]), pool_t(past["slc_v"]), q8, gates8, kc, vc,
                        pool_t(past["win_k"]), pool_t(past["win_v"]), new_nsa).reshape(b, NSA_HEADS * d)
    new_moba = jnp.pad(jnp.stack([col(E_KM, hm), col(E_VM, hm)], axis=1), ((0, 0), (0, 6), (0, 0)))
    o_moba = _moba_decode(pt, pool_t(past["moba_k"]), pool_t(past["moba_v"]),
                          col(E_QM, hm).reshape(b, 1, hm), new_moba).reshape(b, hm)
    x = _oproj(x, [o_nsa, o_moba], [prm["even_wo_nsa"], prm["even_wo_moba"]])
    x = _ffn(x, prm["ffn2_norm"][0], prm["ffn2_w_in"][0], prm["ffn2_w_out"][0])
    roll = lambda win, new: jnp.concatenate([win[:, 1:], new.reshape(b, 1, g_n, d)], axis=1)
    r4 = lambda a, h: a.reshape(b, 1, h, a.shape[-1] // h)
    even = (r4(col(E_KC, gd), g_n), r4(col(E_VC, gd), g_n), r4(col(E_KS, gd), g_n), r4(col(E_VS, gd), g_n),
            r4(col(E_KM, hm), MOBA_HEADS), r4(col(E_VM, hm), MOBA_HEADS),
            roll(past["win_k"], col(E_KW, gd)), roll(past["win_v"], col(E_VW, gd)))
    x = _ffn(x, prm["ffn1_norm"][1], prm["ffn1_w_in"][1], prm["ffn1_w_out"][1])
    po = _proj(x, prm["mix_norm"][1], prm["odd_w"], prm["odd_gain"], O_FLAGS)
    heads = lambda a: a.reshape(b, DIFF_HEADS, 2 * d)
    o_diff = _diff_decode(pt, past["diff_k"], past["diff_v"], heads(po[:, :O_K]), heads(po[:, O_K:O_V]),
                          heads(po[:, O_V:]), prm["lam_rows"], prm["subln_gain"],
                          prm["lambda_init"]).reshape(b, DIFF_HEADS * 2 * d)
    x = _oproj(x, [o_diff], [prm["odd_wo"]])
    x = _ffn(x, prm["ffn2_norm"][1], prm["ffn2_w_in"][1], prm["ffn2_w_out"][1])
    odd = (r4(po[:, O_K:O_V], DIFF_HEADS), r4(po[:, O_V:], DIFF_HEADS))
    return x.reshape(b, t, dm), even, odd


def kernel(x_prompt, x_sample, cache_nsa_cmp_k, cache_nsa_cmp_v, cache_nsa_slc_k, cache_nsa_slc_v,
           cache_moba_k, cache_moba_v, state_nsa_win_k, state_nsa_win_v, cache_diff_k, cache_diff_v,
           page_table, ffn1_norm, ffn1_w_in, ffn1_w_out, mix_norm, ffn2_norm, ffn2_w_in, ffn2_w_out,
           even_w_in, even_w_out, nsa_q_gain, nsa_kc_gain, nsa_ks_gain, nsa_kw_gain,
           cmp_k_pe, cmp_k_w1, cmp_k_w2, cmp_v_pe, cmp_v_w1, cmp_v_w2, moba_q_gain, moba_k_gain,
           odd_w_in, odd_w_out, diff_q_gain, diff_k_gain, diff_lq1, diff_lk1, diff_lq2, diff_lk2,
           diff_subln_gain):
    even_w, even_gain, wo_nsa, wo_moba = _even_weights(even_w_in[0], even_w_out[0], nsa_q_gain[0], nsa_ks_gain[0],
                                                       nsa_kw_gain[0], moba_q_gain[0], moba_k_gain[0])
    odd_w, odd_gain, odd_wo = _odd_weights(odd_w_in[0], odd_w_out[0], diff_q_gain[0], diff_k_gain[0])
    lam_rows = jnp.concatenate([diff_lq1, diff_lk1, diff_lq2, diff_lk2, jnp.zeros((4, HEAD_DIM), F32)], axis=0)
    prm = {
        "ffn1_norm": ffn1_norm, "ffn1_w_in": ffn1_w_in.astype(BF16), "ffn1_w_out": ffn1_w_out.astype(BF16),
        "ffn2_norm": ffn2_norm, "ffn2_w_in": ffn2_w_in.astype(BF16), "ffn2_w_out": ffn2_w_out.astype(BF16),
        "mix_norm": mix_norm,
        "even_w": even_w.astype(BF16), "even_gain": even_gain, "even_wo_nsa": wo_nsa, "even_wo_moba": wo_moba,
        "even_w32": even_w, "ffn1_w_in32": ffn1_w_in, "ffn1_w_out32": ffn1_w_out,
        "cmp_k": _cmp_weights(cmp_k_pe[0], cmp_k_w1[0], cmp_k_w2[0]),
        "cmp_v": _cmp_weights(cmp_v_pe[0], cmp_v_w1[0], cmp_v_w2[0]),
        "nsa_kc_gain": nsa_kc_gain[0],
        "odd_w": odd_w, "odd_gain": odd_gain, "odd_wo": odd_wo,
        "lam_rows": lam_rows, "subln_gain": diff_subln_gain[0],
        "lambda_init": 0.8 - 0.6 * math.exp(-0.3 * 1),
    }
    past = {
        "page_table": page_table,
        "cmp_k": cache_nsa_cmp_k[0], "cmp_v": cache_nsa_cmp_v[0],
        "slc_k": cache_nsa_slc_k[0], "slc_v": cache_nsa_slc_v[0],
        "moba_k": cache_moba_k[0], "moba_v": cache_moba_v[0],
        "win_k": state_nsa_win_k[0], "win_v": state_nsa_win_v[0],
        "diff_k": cache_diff_k[0], "diff_v": cache_diff_v[0],
    }
    y_p, even_p, odd_p = _trunk_prompt(x_prompt, prm)
    y_s, even_s, odd_s = _trunk_sample(x_sample, prm, past)
    outs = [y_p, y_s]
    for p_leaf, s_leaf in zip(even_p + odd_p, even_s + odd_s):
        outs += [p_leaf[None], s_leaf[None]]
    return tuple(outs)
```

```python
import functools
import math

import numpy as np
import jax
import jax.numpy as jnp
from jax import lax
from jax.experimental import pallas as pl
from jax.experimental.pallas import tpu as pltpu

F32 = jnp.float32
BF16 = jnp.bfloat16
HIGHEST = lax.Precision.HIGHEST

D_MODEL = 1024
HEAD_DIM = 64
NSA_HEADS = 8
NSA_KV_HEADS = 2
NSA_GROUP = NSA_HEADS // NSA_KV_HEADS
CMP_BLOCK = 32
CMP_STRIDE = 16
CMP_HIDDEN = 2 * HEAD_DIM
SLC_BLOCK = 64
SLC_TOP = 3
WINDOW = 256
MOBA_HEADS = 8
MOBA_BLOCK = 256
MOBA_TOP = 3
DIFF_HEADS = D_MODEL // (2 * HEAD_DIM)
PAGE_SIZE = 128
RMS_EPS = 1e-6
NEG_INF = -1e30
QK_SCALE = HEAD_DIM ** -0.5
LANES = 128
MXU_N = 256
VMEM_LIMIT = 56 * 1024 * 1024

E_QN, E_KS, E_KW, E_QM, E_KM = 0, 512, 640, 768, 1280
E_KC, E_VC, E_VS, E_VW, E_VM, E_GT = 1792, 1920, 2048, 2176, 2304, 2816
E_COLS = 3072
E_FLAGS = (1, 1, 1, 1, 1, 1, 1, 0, 0, 0, 0, 0)
O_Q, O_K, O_V = 0, 1024, 2048
O_COLS = 3072
O_FLAGS = (1, 1, 1, 1, 1, 1, 1, 1, 0, 0, 0, 0)


def _nt_dot(a, b, precision=None):
    return lax.dot_general(a, b, (((1,), (1,)), ((), ())), precision=precision,
                           preferred_element_type=F32)


def _dot(a, b, precision=None):
    return jnp.dot(a, b, precision=precision, preferred_element_type=F32)


def _mm(a, w):
    if w.dtype == F32:
        return _dot(a.astype(F32), w, precision=HIGHEST)
    return _dot(a.astype(w.dtype), w)


def _rms(x, gain):
    return x * lax.rsqrt(jnp.mean(x * x, axis=-1, keepdims=True) + RMS_EPS) * gain


def _params(*sem):
    return pltpu.CompilerParams(dimension_semantics=sem, vmem_limit_bytes=VMEM_LIMIT)


def _ffn_kernel(x_ref, g_ref, wg_ref, wu_ref, wo_ref, o_ref, xn_ref, acc_ref):
    j = pl.program_id(1)

    @pl.when(j == 0)
    def _():
        xn_ref[...] = _rms(x_ref[...], g_ref[...]).astype(xn_ref.dtype)
        acc_ref[...] = jnp.zeros_like(acc_ref)

    xn = xn_ref[...]
    gate = _mm(xn, wg_ref[...])
    up = _mm(xn, wu_ref[...])
    acc_ref[...] += _mm(gate * jax.nn.sigmoid(gate) * up, wo_ref[...])

    @pl.when(j == pl.num_programs(1) - 1)
    def _():
        o_ref[...] = x_ref[...] + 0.5 * acc_ref[...]


def _ffn(x, gain, w_in, w_out):
    m, d = x.shape
    f = w_out.shape[0]
    tm = min(m, 512)
    tf = f // 2
    nf = f // tf
    return pl.pallas_call(
        _ffn_kernel,
        grid=(m // tm, nf),
        in_specs=[
            pl.BlockSpec((tm, d), lambda i, j: (i, 0)),
            pl.BlockSpec((1, d), lambda i, j: (0, 0)),
            pl.BlockSpec((d, tf), lambda i, j: (0, j)),
            pl.BlockSpec((d, tf), lambda i, j: (0, j + nf)),
            pl.BlockSpec((tf, d), lambda i, j: (j, 0)),
        ],
        out_specs=pl.BlockSpec((tm, d), lambda i, j: (i, 0)),
        out_shape=jax.ShapeDtypeStruct((m, d), F32),
        scratch_shapes=[pltpu.VMEM((tm, d), w_in.dtype), pltpu.VMEM((tm, d), F32)],
        compiler_params=_params("parallel", "arbitrary"),
        name="ffn",
    )(x, gain.reshape(1, d), w_in, w_in, w_out)


def _proj_kernel(flag_ref, x_ref, g_ref, w_ref, cg_ref, bd_ref, o_ref, xn_ref):
    j = pl.program_id(1)

    @pl.when(j == 0)
    def _():
        xn_ref[...] = _rms(x_ref[...], g_ref[...]).astype(xn_ref.dtype)

    y = _mm(xn_ref[...], w_ref[...])

    @pl.when(flag_ref[j] == 1)
    def _():
        y2 = y * y
        hi = y2.astype(BF16)
        lo = (y2 - hi.astype(F32)).astype(BF16)
        ss = _dot(hi, bd_ref[...]) + _dot(lo, bd_ref[...])
        o_ref[...] = y * lax.rsqrt(ss * (1.0 / HEAD_DIM) + RMS_EPS) * cg_ref[...]

    @pl.when(flag_ref[j] == 0)
    def _():
        o_ref[...] = y


def _proj(x, gain, w, col_gain, flags):
    m, d = x.shape
    n = w.shape[1]
    tm = min(m, 1024)
    tn = MXU_N
    head = np.arange(tn) // HEAD_DIM
    bd = jnp.asarray(head[:, None] == head[None, :], BF16)
    return pl.pallas_call(
        _proj_kernel,
        grid_spec=pltpu.PrefetchScalarGridSpec(
            num_scalar_prefetch=1,
            grid=(m // tm, n // tn),
            in_specs=[
                pl.BlockSpec((tm, d), lambda i, j, fl: (i, 0)),
                pl.BlockSpec((1, d), lambda i, j, fl: (0, 0)),
                pl.BlockSpec((d, tn), lambda i, j, fl: (0, j)),
                pl.BlockSpec((1, tn), lambda i, j, fl: (0, j)),
                pl.BlockSpec((tn, tn), lambda i, j, fl: (0, 0)),
            ],
            out_specs=pl.BlockSpec((tm, tn), lambda i, j, fl: (i, j)),
            scratch_shapes=[pltpu.VMEM((tm, d), w.dtype)],
        ),
        out_shape=jax.ShapeDtypeStruct((m, n), F32),
        compiler_params=_params("parallel", "arbitrary"),
        name="proj",
    )(jnp.asarray(flags, jnp.int32), x, gain.reshape(1, d), w, col_gain.reshape(1, n), bd)


def _oproj_kernel(n_in, x_ref, *refs):
    o_ref = refs[2 * n_in]
    acc = x_ref[...]
    for a_ref, w_ref in zip(refs[:n_in], refs[n_in:2 * n_in]):
        acc = acc + _dot(a_ref[...].astype(BF16), w_ref[...])
    o_ref[...] = acc


def _oproj(x, acts, ws):
    m, d = x.shape
    tm = min(m, 512)
    n_in = len(acts)
    in_specs = [pl.BlockSpec((tm, d), lambda i: (i, 0))]
    in_specs += [pl.BlockSpec((tm, a.shape[1]), lambda i: (i, 0)) for a in acts]
    in_specs += [pl.BlockSpec(w.shape, lambda i: (0, 0)) for w in ws]
    return pl.pallas_call(
        functools.partial(_oproj_kernel, n_in),
        grid=(m // tm,),
        in_specs=in_specs,
        out_specs=pl.BlockSpec((tm, d), lambda i: (i, 0)),
        out_shape=jax.ShapeDtypeStruct((m, d), F32),
        compiler_params=_params("parallel"),
        name="oproj",
    )(x, *acts, *ws)


def _top_select(cur, lane, own, n_top):
    sel = jnp.zeros(cur.shape, F32)
    for _ in range(n_top):
        mx = jnp.max(cur, axis=-1, keepdims=True)
        idx = jnp.min(jnp.where(cur == mx, lane, 4 * LANES), axis=-1, keepdims=True)
        pick = lane == idx
        sel = jnp.where(pick, jnp.where(lane < own, 1.0, 0.0), sel)
        cur = jnp.where(pick, -3e38, cur)
    return sel


def _softmax_update(carry, s, mask, v):
    m, l, acc = carry
    s = jnp.where(mask, s, NEG_INF)
    m_new = jnp.maximum(m, jnp.max(s, axis=-1, keepdims=True))
    alpha = jnp.exp(m - m_new)
    p = jnp.where(mask, jnp.exp(s - m_new), 0.0)
    l = alpha * l + jnp.sum(p, axis=-1, keepdims=True)
    acc = alpha * acc + _dot(p.astype(BF16), v)
    return m_new, l, acc


def _softmax_init(rows, width):
    return (jnp.full((rows, 1), NEG_INF, F32), jnp.zeros((rows, 1), F32), jnp.zeros((rows, width), F32))


def _safe_div(acc, l):
    return jnp.where(l > 0.0, acc / jnp.where(l > 0.0, l, 1.0), 0.0)


def _compress_rows(scr, n_c, w1_ref, w2_ref, pe_ref):
    x = jnp.concatenate([scr[pl.ds(r, n_c, stride=CMP_STRIDE), :] for r in range(CMP_BLOCK)], axis=1)
    hid = _dot(x.astype(BF16), w1_ref[...]) + _dot(pe_ref[...].astype(BF16), w1_ref[...])[0:1]
    hid = hid * jax.nn.sigmoid(hid)
    return _dot(hid.astype(BF16), w2_ref[...])


def _store_kc(kc, gk_ref, kc_ref):
    g = gk_ref[...]
    for grp in range(NSA_KV_HEADS):
        sl = slice(grp * HEAD_DIM, (grp + 1) * HEAD_DIM)
        kc_ref[0, :, sl] = _rms(kc[:, sl], g[:, sl])


def _cmp_prompt_kernel(k_ref, v_ref, w1k, w2k, pek, gk, w1v, w2v, pev, kc_ref, vc_ref, scr):
    t = k_ref.shape[0]
    n_c = t // CMP_STRIDE
    scr[t:, :] = jnp.zeros((scr.shape[0] - t, scr.shape[1]), F32)
    scr[:t, :] = k_ref[...]
    _store_kc(_compress_rows(scr, n_c, w1k, w2k, pek), gk, kc_ref)
    scr[:t, :] = v_ref[...]
    vc_ref[0] = _compress_rows(scr, n_c, w1v, w2v, pev)


def _cmp_decode_kernel(n_pages, pt_ref, *refs):
    k_pages = refs[:n_pages]
    v_pages = refs[n_pages:2 * n_pages]
    (w1k, w2k, pek, gk, w1v, w2v, pev, kc_ref, vc_ref, scr) = refs[2 * n_pages:]
    t = n_pages * PAGE_SIZE
    n_c = t // CMP_STRIDE
    scr[t:, :] = jnp.zeros((scr.shape[0] - t, scr.shape[1]), F32)
    for p, pg in enumerate(k_pages):
        scr[p * PAGE_SIZE:(p + 1) * PAGE_SIZE, :] = pg[0].T
    _store_kc(_compress_rows(scr, n_c, w1k, w2k, pek), gk, kc_ref)
    for p, pg in enumerate(v_pages):
        scr[p * PAGE_SIZE:(p + 1) * PAGE_SIZE, :] = pg[0].T
    vc_ref[0] = _compress_rows(scr, n_c, w1v, w2v, pev)


def _cmp_weights(pe, w1, w2):
    d, hdn, g_n = HEAD_DIM, CMP_HIDDEN, NSA_KV_HEADS
    w1r = w1.reshape(CMP_BLOCK, d, hdn)
    eye = jnp.eye(g_n, dtype=F32)
    w1b = jnp.einsum("rdk,gq->rgdqk", w1r, eye).reshape(CMP_BLOCK * g_n * d, g_n * hdn)
    w2b = jnp.einsum("kd,gq->gkqd", w2, eye).reshape(g_n * hdn, g_n * d)
    per = jnp.broadcast_to(pe.reshape(CMP_BLOCK, 1, d), (CMP_BLOCK, g_n, d)).reshape(1, -1)
    pe8 = jnp.concatenate([per, jnp.zeros((7, per.shape[1]), F32)], axis=0)
    return w1b.astype(BF16), w2b.astype(BF16), pe8


def _cmp_consts(wk, wv, kc_gain):
    gk = jnp.tile(kc_gain, NSA_KV_HEADS).reshape(1, NSA_KV_HEADS * HEAD_DIM)
    return [*wk, gk, *wv]


def _compress_prompt(p_even, wk, wv, kc_gain, b, t):
    n_c = t // CMP_STRIDE
    gd = NSA_KV_HEADS * HEAD_DIM
    consts = _cmp_consts(wk, wv, kc_gain)
    return pl.pallas_call(
        _cmp_prompt_kernel,
        grid=(b,),
        in_specs=[pl.BlockSpec((t, gd), lambda i: (i, E_KC // gd)), pl.BlockSpec((t, gd), lambda i: (i, E_VC // gd))]
        + [pl.BlockSpec(a.shape, lambda i: (0, 0)) for a in consts],
        out_specs=[pl.BlockSpec((1, n_c, gd), lambda i: (i, 0, 0))] * 2,
        out_shape=[jax.ShapeDtypeStruct((b, n_c, gd), F32)] * 2,
        scratch_shapes=[pltpu.VMEM((t + CMP_STRIDE, gd), F32)],
        compiler_params=_params("parallel"),
        name="nsa_compress_prompt",
    )(p_even, p_even, *consts)


def _compress_decode(page_table, k_pool_t, v_pool_t, wk, wv, kc_gain):
    b, n_pages = page_table.shape
    t = n_pages * PAGE_SIZE
    n_c = t // CMP_STRIDE
    gd = NSA_KV_HEADS * HEAD_DIM
    page_specs = [pl.BlockSpec((1, gd, PAGE_SIZE), lambda i, pt, p=p: (pt[i * n_pages + p], 0, 0))
                  for p in range(n_pages)]
    consts = _cmp_consts(wk, wv, kc_gain)
    return pl.pallas_call(
        functools.partial(_cmp_decode_kernel, n_pages),
        grid_spec=pltpu.PrefetchScalarGridSpec(
            num_scalar_prefetch=1,
            grid=(b,),
            in_specs=page_specs + page_specs + [pl.BlockSpec(a.shape, lambda i, pt: (0, 0)) for a in consts],
            out_specs=[pl.BlockSpec((1, n_c, gd), lambda i, pt: (i, 0, 0))] * 2,
            scratch_shapes=[pltpu.VMEM((t + CMP_STRIDE, gd), F32)],
        ),
        out_shape=[jax.ShapeDtypeStruct((b, n_c, gd), F32)] * 2,
        compiler_params=_params("parallel"),
        name="nsa_compress_decode",
    )(page_table.reshape(-1), *([k_pool_t] * n_pages), *([v_pool_t] * n_pages), *consts)


def _cover_matrix():
    j = np.arange(LANES)[:, None]
    n = np.arange(LANES)[None, :]
    c0 = j * CMP_STRIDE
    s0 = n * SLC_BLOCK
    return jnp.asarray((c0 < s0 + SLC_BLOCK) & (c0 + CMP_BLOCK - 1 >= s0), F32)


def _expand_matrix(block, t):
    n = np.arange(LANES)[:, None]
    s = np.arange(t)[None, :]
    return jnp.asarray(s // block == n, BF16)


def _nsa_kernel(q_ref, gt_ref, ksw_ref, vsw_ref, kc_ref, vc_ref, cov_ref, exp_ref, o_ref, selk_ref):
    i = pl.program_id(1)
    tq_n = q_ref.shape[0]
    r_n = NSA_GROUP
    d = HEAD_DIM
    t0 = i * tq_n
    rows = r_n * tq_n
    row = lax.broadcasted_iota(jnp.int32, (rows, 1), 0)
    tq = t0 + row % tq_n
    tq1 = tq[:tq_n]
    lane = lax.broadcasted_iota(jnp.int32, (1, LANES), 1)
    gates = jax.nn.sigmoid(gt_ref[:, :LANES])
    own = tq1 // SLC_BLOCK

    for g in range(NSA_KV_HEADS):
        kcol = slice(g * d, (g + 1) * d)
        wcol = slice(LANES + g * d, LANES + (g + 1) * d)
        qg = jnp.concatenate([q_ref[:, (g * r_n + r) * d:(g * r_n + r + 1) * d] for r in range(r_n)], axis=0)
        qg = (qg * QK_SCALE).astype(BF16)
        slope = jnp.exp2(-(g * r_n + row // tq_n + 1).astype(F32))

        c_end = lane * CMP_STRIDE + (CMP_BLOCK - 1)
        s = _nt_dot(qg, kc_ref[0, :, kcol].astype(BF16)) - slope * (tq - c_end).astype(F32)
        valid = c_end <= tq
        s = jnp.where(valid, s, NEG_INF)
        p_c = jnp.where(valid, jnp.exp(s - jnp.max(s, axis=-1, keepdims=True)), 0.0)
        p_c = _safe_div(p_c, jnp.sum(p_c, axis=-1, keepdims=True))
        o_c = _dot(p_c.astype(BF16), vc_ref[0, :, kcol].astype(BF16))

        p_sum = p_c[0:tq_n]
        for r in range(1, r_n):
            p_sum = p_sum + p_c[r * tq_n:(r + 1) * tq_n]
        imp = _dot(p_sum, cov_ref[...], precision=HIGHEST)
        cur = jnp.where(lane < own, imp, -1.0)
        sel = _top_select(cur, lane, own, SLC_TOP)
        sel = jnp.where(lane == own, 1.0, sel)
        selk_ref[...] = _dot(sel.astype(BF16), exp_ref[...])

        def slc_body(j, carry):
            k0 = pl.multiple_of(j * tq_n, tq_n)
            kt = ksw_ref[pl.ds(k0, tq_n), kcol].astype(BF16)
            vt = vsw_ref[pl.ds(k0, tq_n), kcol].astype(BF16)
            pos = k0 + lane
            s = _nt_dot(qg, kt) - slope * (tq - pos).astype(F32)
            mk = selk_ref[:, pl.ds(k0, tq_n)]
            mk = jnp.concatenate([mk] * r_n, axis=0) > 0.5
            return _softmax_update(carry, s, mk & (pos <= tq), vt)

        _, l_s, acc_s = lax.fori_loop(0, i + 1, slc_body, _softmax_init(rows, d))
        o_s = _safe_div(acc_s, l_s)

        carry = _softmax_init(rows, d)
        for jj in range(WINDOW // tq_n + 1):
            tile = i - WINDOW // tq_n + jj
            k0 = pl.multiple_of(jnp.maximum(tile, 0) * tq_n, tq_n)
            kt = ksw_ref[pl.ds(k0, tq_n), wcol].astype(BF16)
            vt = vsw_ref[pl.ds(k0, tq_n), wcol].astype(BF16)
            pos = tile * tq_n + lane
            dist = tq - pos
            s = _nt_dot(qg, kt) - slope * dist.astype(F32)
            carry = _softmax_update(carry, s, (pos >= 0) & (dist >= 0) & (dist <= WINDOW), vt)
        o_w = _safe_div(carry[2], carry[1])

        for r in range(r_n):
            h = g * r_n + r
            rs = slice(r * tq_n, (r + 1) * tq_n)
            o_ref[:, h * d:(h + 1) * d] = (gates[:, h:h + 1] * o_c[rs]
                                           + gates[:, NSA_HEADS + h:NSA_HEADS + h + 1] * o_s[rs]
                                           + gates[:, 2 * NSA_HEADS + h:2 * NSA_HEADS + h + 1] * o_w[rs])


def _nsa_prompt(p_even, kc, vc, b, t):
    tq_n = LANES
    nq = t // tq_n
    cov = _cover_matrix()
    exp = _expand_matrix(SLC_BLOCK, t)
    return pl.pallas_call(
        _nsa_kernel,
        grid=(b, nq),
        in_specs=[
            pl.BlockSpec((tq_n, NSA_HEADS * HEAD_DIM), lambda bi, i: (bi * nq + i, E_QN // 512)),
            pl.BlockSpec((tq_n, 2 * LANES), lambda bi, i: (bi * nq + i, E_GT // 256)),
            pl.BlockSpec((t, 2 * LANES), lambda bi, i: (bi, E_KS // 256)),
            pl.BlockSpec((t, 2 * LANES), lambda bi, i: (bi, E_VS // 256)),
            pl.BlockSpec((1, LANES, LANES), lambda bi, i: (bi, 0, 0)),
            pl.BlockSpec((1, LANES, LANES), lambda bi, i: (bi, 0, 0)),
            pl.BlockSpec((LANES, LANES), lambda bi, i: (0, 0)),
            pl.BlockSpec((LANES, t), lambda bi, i: (0, 0)),
        ],
        out_specs=pl.BlockSpec((tq_n, NSA_HEADS * HEAD_DIM), lambda bi, i: (bi * nq + i, 0)),
        out_shape=jax.ShapeDtypeStruct((b * t, NSA_HEADS * HEAD_DIM), F32),
        scratch_shapes=[pltpu.VMEM((tq_n, t), F32)],
        compiler_params=_params("parallel", "arbitrary"),
        name="nsa_prompt",
    )(p_even, p_even, p_even, p_even, kc, vc, cov, exp)


def _moba_kernel(q_ref, k_ref, v_ref, o_ref, km_ref):
    hp = pl.program_id(1)
    i = pl.program_id(2)
    tq_n = q_ref.shape[0]
    t = k_ref.shape[0]
    n_b = t // MOBA_BLOCK
    d = HEAD_DIM

    @pl.when(i == 0)
    def _():
        km_ref[...] = jnp.zeros_like(km_ref)
        for n in range(n_b):
            km_ref[n:n + 1, :] = jnp.mean(k_ref[n * MOBA_BLOCK:(n + 1) * MOBA_BLOCK, :], axis=0, keepdims=True)

    t0 = i * tq_n
    tq = t0 + lax.broadcasted_iota(jnp.int32, (tq_n, 1), 0)
    lane = lax.broadcasted_iota(jnp.int32, (1, LANES), 1)
    kpos = lax.broadcasted_iota(jnp.int32, (1, MOBA_BLOCK), 1)
    own = i

    for hh in range(2):
        col = slice(hh * d, (hh + 1) * d)
        hvec = jnp.zeros((1, 1), jnp.int32) + (hp * 2 + hh + 1)
        slope = jnp.exp2(-hvec.astype(F32))
        q = q_ref[:, col]
        route = _nt_dot(q, km_ref[:, col], precision=HIGHEST)
        cur = jnp.where(lane < own, route, NEG_INF)
        cur = jnp.where(lane < n_b, cur, -2e38)
        sel = _top_select(cur, lane, own, MOBA_TOP)
        sel = jnp.where(lane == own, 1.0, sel)
        qb = (q * QK_SCALE).astype(BF16)

        def body(n, carry):
            k0 = pl.multiple_of(n * MOBA_BLOCK, MOBA_BLOCK)
            kt = k_ref[pl.ds(k0, MOBA_BLOCK), col].astype(BF16)
            vt = v_ref[pl.ds(k0, MOBA_BLOCK), col].astype(BF16)
            pos = k0 + kpos
            s = _nt_dot(qb, kt) - slope * (tq - pos).astype(F32)
            chosen = jnp.sum(jnp.where(lane == n, sel, 0.0), axis=-1, keepdims=True) > 0.5
            return _softmax_update(carry, s, chosen & (pos <= tq), vt)

        _, l, acc = lax.fori_loop(0, i + 1, body, _softmax_init(tq_n, d))
        o_ref[:, col] = _safe_div(acc, l)


def _moba_prompt(p_even, b, t):
    tq_n = MOBA_BLOCK
    nq = t // tq_n
    return pl.pallas_call(
        _moba_kernel,
        grid=(b, MOBA_HEADS // 2, nq),
        in_specs=[
            pl.BlockSpec((tq_n, LANES), lambda bi, hp, i: (bi * nq + i, E_QM // LANES + hp)),
            pl.BlockSpec((t, LANES), lambda bi, hp, i: (bi, E_KM // LANES + hp)),
            pl.BlockSpec((t, LANES), lambda bi, hp, i: (bi, E_VM // LANES + hp)),
        ],
        out_specs=pl.BlockSpec((tq_n, LANES), lambda bi, hp, i: (bi * nq + i, hp)),
        out_shape=jax.ShapeDtypeStruct((b * t, MOBA_HEADS * HEAD_DIM), F32),
        scratch_shapes=[pltpu.VMEM((LANES, LANES), F32)],
        compiler_params=_params("parallel", "parallel", "arbitrary"),
        name="moba_prompt",
    )(p_even, p_even, p_even)


def _diff_lambda(lam_ref, lambda_init):
    lv = lam_ref[...]
    a = jnp.sum(lv[0:1] * lv[1:2], axis=-1, keepdims=True)
    c = jnp.sum(lv[2:3] * lv[3:4], axis=-1, keepdims=True)
    return jnp.exp(a) - jnp.exp(c) + lambda_init


def _diff_kernel(lambda_init, q_ref, k_ref, v_ref, lam_ref, sg_ref, o_ref):
    h = pl.program_id(1)
    i = pl.program_id(2)
    tq_n = q_ref.shape[0]
    d2 = 2 * HEAD_DIM
    t0 = i * tq_n
    row = lax.broadcasted_iota(jnp.int32, (2 * tq_n, 1), 0)
    tq = t0 + row % tq_n
    lane = lax.broadcasted_iota(jnp.int32, (1, d2), 1)
    kpos = lax.broadcasted_iota(jnp.int32, (1, tq_n), 1)
    slope = jnp.exp2(-(jnp.zeros((1, 1), jnp.int32) + (h + 1)).astype(F32))
    q = q_ref[...] * QK_SCALE
    q2 = jnp.concatenate([jnp.where(lane < HEAD_DIM, q, 0.0), jnp.where(lane >= HEAD_DIM, q, 0.0)], axis=0)
    q2 = q2.astype(BF16)

    def body(j, carry):
        k0 = pl.multiple_of(j * tq_n, tq_n)
        kt = k_ref[pl.ds(k0, tq_n), :].astype(BF16)
        vt = v_ref[pl.ds(k0, tq_n), :].astype(BF16)
        pos = k0 + kpos
        s = _nt_dot(q2, kt) - slope * (tq - pos).astype(F32)
        return _softmax_update(carry, s, pos <= tq, vt)

    _, l, acc = lax.fori_loop(0, i + 1, body, _softmax_init(2 * tq_n, d2))
    o = acc / l
    lam = _diff_lambda(lam_ref, lambda_init)
    attn = o[:tq_n] - lam * o[tq_n:]
    o_ref[...] = _rms(attn, sg_ref[...]) * (1.0 - lambda_init)


def _diff_prompt(p_odd, lam_rows, subln_gain, lambda_init, b, t):
    tq_n = 256
    nq = t // tq_n
    d2 = 2 * HEAD_DIM
    return pl.pallas_call(
        functools.partial(_diff_kernel, lambda_init),
        grid=(b, DIFF_HEADS, nq),
        in_specs=[
            pl.BlockSpec((tq_n, d2), lambda bi, h, i: (bi * nq + i, O_Q // d2 + h)),
            pl.BlockSpec((t, d2), lambda bi, h, i: (bi, O_K // d2 + h)),
            pl.BlockSpec((t, d2), lambda bi, h, i: (bi, O_V // d2 + h)),
            pl.BlockSpec(lam_rows.shape, lambda bi, h, i: (0, 0)),
            pl.BlockSpec((1, d2), lambda bi, h, i: (0, 0)),
        ],
        out_specs=pl.BlockSpec((tq_n, d2), lambda bi, h, i: (bi * nq + i, h)),
        out_shape=jax.ShapeDtypeStruct((b * t, DIFF_HEADS * d2), F32),
        compiler_params=_params("parallel", "parallel", "arbitrary"),
        name="diff_prompt",
    )(p_odd, p_odd, p_odd, lam_rows, subln_gain.reshape(1, d2))


def _tflash_init(dv, n):
    return (jnp.full((1, n), NEG_INF, F32), jnp.zeros((1, n), F32), jnp.zeros((dv, n), F32))


def _tflash_step(state, kt, qt, bias, vts, mask=None):
    m, l, acc = state
    st = _dot(kt, qt) + bias
    if mask is not None:
        st = jnp.where(mask, st, NEG_INF)
    m_new = jnp.maximum(m, jnp.max(st, axis=0, keepdims=True))
    alpha = jnp.exp(m - m_new)
    p = jnp.exp(st - m_new)
    l = alpha * l + jnp.sum(p, axis=0, keepdims=True)
    pb = p.astype(BF16)
    seg = pb.shape[1] // len(vts)
    pv = [_dot(vt, pb[:, s * seg:(s + 1) * seg]) for s, vt in enumerate(vts)]
    acc = alpha * acc + (pv[0] if len(pv) == 1 else jnp.concatenate(pv, axis=1))
    return m_new, l, acc


def _top_select_rows(cur, row, own, n_top):
    sel = jnp.zeros(cur.shape, F32)
    for _ in range(n_top):
        mx = jnp.max(cur, axis=0, keepdims=True)
        idx = jnp.min(jnp.where(cur == mx, row, 4 * LANES), axis=0, keepdims=True)
        pick = row == idx
        sel = jnp.where(pick, jnp.where(row < own, 1.0, 0.0), sel)
        cur = jnp.where(pick, -3e38, cur)
    return sel


def _transpose_into(dst_ref, src_ref, col0):
    for n in range(src_ref.shape[0] // LANES):
        dst_ref[:, n * LANES:(n + 1) * LANES] = src_ref[n * LANES:(n + 1) * LANES, col0:col0 + LANES].T


def _lane_tile(x, n):
    return x if n == 1 else jnp.concatenate([x] * n, axis=1)


def _pos_lanes(t):
    return jnp.asarray(np.broadcast_to(np.arange(t, dtype=np.float32)[:, None], (t, LANES)))


def _nsa_attn_kernel(q_ref, gt_ref, ksw_ref, vsw_ref, kc_ref, vc_ref, covt_ref, expt_ref, pos_ref, o_ref,
                     vst_ref, vwt_ref, vct_ref, selk_ref):
    i = pl.program_id(1)
    nq = q_ref.shape[0]
    d, r_n = HEAD_DIM, NSA_GROUP
    n_cb = NSA_HEADS * d // LANES

    @pl.when(i == 0)
    def _():
        _transpose_into(vst_ref, vsw_ref, 0)
        _transpose_into(vwt_ref, vsw_ref, LANES)
        vct_ref[...] = vc_ref[0].T

    t0 = i * nq
    lane = lax.broadcasted_iota(jnp.int32, (1, nq), 1)
    row = lax.broadcasted_iota(jnp.int32, (LANES, 1), 0)
    tq = t0 + lane
    own = tq // SLC_BLOCK
    tk = 4 * LANES
    krow = lax.broadcasted_iota(jnp.int32, (tk, 1), 0)
    gates_t = jax.nn.sigmoid(gt_ref[:, :LANES]).T
    q_t = [q_ref[:, cb * LANES:(cb + 1) * LANES].T for cb in range(n_cb)]
    c_end = _lane_tile(pos_ref[0:LANES, :], nq // LANES) * CMP_STRIDE + (CMP_BLOCK - 1)
    c_valid = c_end <= tq.astype(F32)
    head_out = [None] * NSA_HEADS

    for g in range(NSA_KV_HEADS):
        g_rows = row // d == g
        heads = [g * r_n + r for r in range(r_n)]
        slopes = [2.0 ** -(h + 1) for h in heads]

        def head_qt(h):
            x = q_t[h // 2]
            if h % 2 != g:
                x = pltpu.roll(x, d, 0)
            return (jnp.where(g_rows, x, 0.0) * QK_SCALE).astype(BF16)

        qt = jnp.concatenate([head_qt(h) for h in heads], axis=1)
        n_all = r_n * nq

        def head_bias(base):
            base = _lane_tile(base, nq // base.shape[1])
            return jnp.concatenate([s * base for s in slopes], axis=1)

        vrows = slice(g * d, (g + 1) * d)

        cm = _lane_tile(c_valid, r_n)
        s = jnp.where(cm, _dot(kc_ref[0].astype(BF16), qt) + head_bias(c_end), NEG_INF)
        p = jnp.where(cm, jnp.exp(s - jnp.max(s, axis=0, keepdims=True)), 0.0)
        l = jnp.sum(p, axis=0, keepdims=True)
        p = p * jnp.where(l > 0.0, 1.0 / jnp.where(l > 0.0, l, 1.0), 0.0)
        o_c = _dot(vct_ref[vrows, :].astype(BF16), p.astype(BF16))
        p_sum = p[:, :nq]
        for r in range(1, r_n):
            p_sum = p_sum + p[:, r * nq:(r + 1) * nq]

        imp = _dot(covt_ref[...], p_sum, precision=HIGHEST)
        cur = jnp.where(row < own, imp, -1.0)
        sel = _top_select_rows(cur, row, own, SLC_TOP)
        sel = jnp.where(row == own, 1.0, sel)
        selk_ref[...] = _dot(expt_ref[...], sel.astype(BF16))

        def slc_tile(state, k0, diag):
            kt = ksw_ref[pl.ds(k0, tk), 0:LANES].astype(BF16)
            vt = vst_ref[vrows, pl.ds(k0, tk)].astype(BF16)
            mk = selk_ref[pl.ds(k0, tk), :] > 0.5
            if diag:
                mk = mk & (k0 + krow <= tq)
            return _tflash_step(state, kt, qt, head_bias(pos_ref[pl.ds(k0, tk), :]), [vt], _lane_tile(mk, r_n))

        j_diag = t0 // tk
        s_state = slc_tile(_tflash_init(d, n_all), pl.multiple_of(j_diag * tk, tk), True)
        _, l_s, acc_s = lax.fori_loop(0, j_diag, lambda j, st: slc_tile(st, pl.multiple_of(j * tk, tk), False),
                                      s_state)

        k0 = pl.multiple_of(jnp.maximum(t0 - WINDOW, 0), nq)
        wrow = k0 + lax.broadcasted_iota(jnp.int32, (WINDOW + nq, 1), 0)
        dist = tq - wrow
        kt = ksw_ref[pl.ds(k0, WINDOW + nq), LANES:2 * LANES].astype(BF16)
        vt = vwt_ref[vrows, pl.ds(k0, WINDOW + nq)].astype(BF16)
        wmask = _lane_tile((dist >= 0) & (dist <= WINDOW), r_n)
        _, l_w, acc_w = _tflash_step(_tflash_init(d, n_all), kt, qt, head_bias(pos_ref[pl.ds(k0, WINDOW + nq), :]),
                                     [vt], wmask)

        o_s = acc_s / l_s
        o_w = acc_w / l_w
        for r, h in enumerate(heads):
            seg = slice(r * nq, (r + 1) * nq)
            head_out[h] = (gates_t[h:h + 1] * o_c[:, seg]
                           + gates_t[NSA_HEADS + h:NSA_HEADS + h + 1] * o_s[:, seg]
                           + gates_t[2 * NSA_HEADS + h:2 * NSA_HEADS + h + 1] * o_w[:, seg])

    for cb in range(n_cb):
        o_ref[:, cb * LANES:(cb + 1) * LANES] = jnp.concatenate([head_out[2 * cb], head_out[2 * cb + 1]], axis=0).T


def _nsa_attn(p_even, kc, vc, b, t):
    nq = 2 * LANES
    n_tiles = t // nq
    covt = _cover_matrix().T
    expt = _expand_matrix(SLC_BLOCK, t).T
    pos = _pos_lanes(t)
    return pl.pallas_call(
        _nsa_attn_kernel,
        grid=(b, n_tiles),
        in_specs=[
            pl.BlockSpec((nq, NSA_HEADS * HEAD_DIM), lambda bi, i: (bi * n_tiles + i, E_QN // 512)),
            pl.BlockSpec((nq, 2 * LANES), lambda bi, i: (bi * n_tiles + i, E_GT // 256)),
            pl.BlockSpec((t, 2 * LANES), lambda bi, i: (bi, E_KS // 256)),
            pl.BlockSpec((t, 2 * LANES), lambda bi, i: (bi, E_VS // 256)),
            pl.BlockSpec((1, LANES, LANES), lambda bi, i: (bi, 0, 0)),
            pl.BlockSpec((1, LANES, LANES), lambda bi, i: (bi, 0, 0)),
            pl.BlockSpec((LANES, LANES), lambda bi, i: (0, 0)),
            pl.BlockSpec((t, LANES), lambda bi, i: (0, 0)),
            pl.BlockSpec((t, LANES), lambda bi, i: (0, 0)),
        ],
        out_specs=pl.BlockSpec((nq, NSA_HEADS * HEAD_DIM), lambda bi, i: (bi * n_tiles + i, 0)),
        out_shape=jax.ShapeDtypeStruct((b * t, NSA_HEADS * HEAD_DIM), F32),
        scratch_shapes=[pltpu.VMEM((LANES, t), F32), pltpu.VMEM((LANES, t), F32), pltpu.VMEM((LANES, LANES), F32),
                        pltpu.VMEM((t, nq), F32)],
        compiler_params=_params("parallel", "arbitrary"),
        name="nsa_attn",
    )(p_even, p_even, p_even, p_even, kc, vc, covt, expt, pos)


def _moba_attn_kernel(q_ref, k_ref, v_ref, pos_ref, o_ref, km_ref, vt_ref):
    hp = pl.program_id(1)
    i = pl.program_id(2)
    nq = q_ref.shape[0]
    n_b = k_ref.shape[0] // MOBA_BLOCK
    d = HEAD_DIM

    @pl.when(i == 0)
    def _():
        for n in range(n_b):
            km_ref[n:n + 1, :] = jnp.mean(k_ref[n * MOBA_BLOCK:(n + 1) * MOBA_BLOCK, :], axis=0, keepdims=True)
        _transpose_into(vt_ref, v_ref, 0)

    bpt = 2
    tk = bpt * MOBA_BLOCK
    lane = lax.broadcasted_iota(jnp.int32, (1, nq), 1)
    row = lax.broadcasted_iota(jnp.int32, (LANES, 1), 0)
    krow = lax.broadcasted_iota(jnp.int32, (tk, 1), 0)
    brow = lax.broadcasted_iota(jnp.int32, (n_b, 1), 0)
    tq = i * nq + lane
    q_t = q_ref[...].T
    qts, sels, biases = [], [], []
    for hh in range(2):
        qh = jnp.where(row // d == hh, q_t, 0.0)
        route = _dot(km_ref[...], qh, precision=HIGHEST)
        cur = jnp.where(brow < i, route, NEG_INF)
        sel = _top_select_rows(cur, brow, i, MOBA_TOP)
        sels.append(jnp.where(brow == i, 1.0, sel))
        qts.append((qh * QK_SCALE).astype(BF16))
        biases.append(jnp.exp2(-(jnp.zeros((1, 1), jnp.int32) + (hp * 2 + hh + 1)).astype(F32)))
    qt = jnp.concatenate(qts, axis=1)

    def tile(state, j, diag):
        k0 = pl.multiple_of(j * tk, tk)
        kt = k_ref[pl.ds(k0, tk), :].astype(BF16)
        base = pos_ref[pl.ds(k0, tk), :]
        vts = [vt_ref[hh * d:(hh + 1) * d, pl.ds(k0, tk)].astype(BF16) for hh in range(2)]
        bias = jnp.concatenate([_lane_tile(biases[hh] * base, nq // LANES) for hh in range(2)], axis=1)
        masks = []
        for hh in range(2):
            chosen = [jnp.max(jnp.where(brow == j * bpt + bb, sels[hh], 0.0), axis=0, keepdims=True)
                      for bb in range(bpt)]
            mk = jnp.where(krow < MOBA_BLOCK, chosen[0], chosen[1]) > 0.5
            if diag:
                mk = mk & (k0 + krow <= tq)
            masks.append(mk)
        return _tflash_step(state, kt, qt, bias, vts, jnp.concatenate(masks, axis=1))

    j_diag = i // bpt
    state = tile(_tflash_init(d, 2 * nq), j_diag, True)
    _, l, acc = lax.fori_loop(0, j_diag, lambda j, st: tile(st, j, False), state)
    o = acc / l
    o_ref[...] = jnp.concatenate([o[:, :nq], o[:, nq:]], axis=0).T


def _moba_attn(p_even, b, t):
    nq = MOBA_BLOCK
    n_tiles = t // nq
    return pl.pallas_call(
        _moba_attn_kernel,
        grid=(b, MOBA_HEADS // 2, n_tiles),
        in_specs=[
            pl.BlockSpec((nq, LANES), lambda bi, hp, i: (bi * n_tiles + i, E_QM // LANES + hp)),
            pl.BlockSpec((t, LANES), lambda bi, hp, i: (bi, E_KM // LANES + hp)),
            pl.BlockSpec((t, LANES), lambda bi, hp, i: (bi, E_VM // LANES + hp)),
            pl.BlockSpec((t, LANES), lambda bi, hp, i: (0, 0)),
        ],
        out_specs=pl.BlockSpec((nq, LANES), lambda bi, hp, i: (bi * n_tiles + i, hp)),
        out_shape=jax.ShapeDtypeStruct((b * t, MOBA_HEADS * HEAD_DIM), F32),
        scratch_shapes=[pltpu.VMEM((t // MOBA_BLOCK, LANES), F32), pltpu.VMEM((LANES, t), F32)],
        compiler_params=_params("parallel", "parallel", "arbitrary"),
        name="moba_attn",
    )(p_even, p_even, p_even, _pos_lanes(t))


def _diff_attn_kernel(lambda_init, q_ref, k_ref, v_ref, pos_ref, lam_ref, sg_ref, o_ref, vt_ref):
    h = pl.program_id(1)
    i = pl.program_id(2)
    nq = q_ref.shape[0]
    d = HEAD_DIM

    @pl.when(i == 0)
    def _():
        _transpose_into(vt_ref, v_ref, 0)

    tk = 4 * LANES
    lane = lax.broadcasted_iota(jnp.int32, (1, nq), 1)
    row = lax.broadcasted_iota(jnp.int32, (LANES, 1), 0)
    krow = lax.broadcasted_iota(jnp.int32, (tk, 1), 0)
    tq = i * nq + lane
    slope = jnp.exp2(-(jnp.zeros((1, 1), jnp.int32) + (h + 1)).astype(F32))
    q_t = q_ref[...].T * QK_SCALE
    qt = jnp.concatenate([jnp.where(row < d, q_t, 0.0), jnp.where(row >= d, q_t, 0.0)], axis=1).astype(BF16)

    def tile(state, j, diag):
        k0 = pl.multiple_of(j * tk, tk)
        kt = k_ref[pl.ds(k0, tk), :].astype(BF16)
        vt = vt_ref[:, pl.ds(k0, tk)].astype(BF16)
        bias = _lane_tile(slope * pos_ref[pl.ds(k0, tk), :], 2 * nq // LANES)
        mask = _lane_tile(k0 + krow <= tq, 2) if diag else None
        return _tflash_step(state, kt, qt, bias, [vt], mask)

    j_diag = i * nq // tk
    state = tile(_tflash_init(2 * d, 2 * nq), j_diag, True)
    _, l, acc = lax.fori_loop(0, j_diag, lambda j, st: tile(st, j, False), state)
    o = acc / l
    attn_t = o[:, :nq] - _diff_lambda(lam_ref, lambda_init) * o[:, nq:]
    o_ref[...] = _rms(attn_t.T, sg_ref[...]) * (1.0 - lambda_init)


def _diff_attn(p_odd, lam_rows, subln_gain, lambda_init, b, t):
    nq = 512
    n_tiles = t // nq
    d2 = 2 * HEAD_DIM
    return pl.pallas_call(
        functools.partial(_diff_attn_kernel, lambda_init),
        grid=(b, DIFF_HEADS, n_tiles),
        in_specs=[
            pl.BlockSpec((nq, d2), lambda bi, h, i: (bi * n_tiles + i, O_Q // d2 + h)),
            pl.BlockSpec((t, d2), lambda bi, h, i: (bi, O_K // d2 + h)),
            pl.BlockSpec((t, d2), lambda bi, h, i: (bi, O_V // d2 + h)),
            pl.BlockSpec((t, LANES), lambda bi, h, i: (0, 0)),
            pl.BlockSpec(lam_rows.shape, lambda bi, h, i: (0, 0)),
            pl.BlockSpec((1, d2), lambda bi, h, i: (0, 0)),
        ],
        out_specs=pl.BlockSpec((nq, d2), lambda bi, h, i: (bi * n_tiles + i, h)),
        out_shape=jax.ShapeDtypeStruct((b * t, DIFF_HEADS * d2), F32),
        scratch_shapes=[pltpu.VMEM((d2, t), F32)],
        compiler_params=_params("parallel", "parallel", "arbitrary"),
        name="diff_attn",
    )(p_odd, p_odd, p_odd, _pos_lanes(t), lam_rows, subln_gain.reshape(1, d2))


def _decode_softmax(s, mask, s_new):
    sm = jnp.where(mask, s, NEG_INF)
    m = jnp.maximum(jnp.max(sm, axis=-1, keepdims=True), s_new)
    p = jnp.where(mask, jnp.exp(sm - m), 0.0)
    p_new = jnp.exp(s_new - m)
    l = jnp.sum(p, axis=-1, keepdims=True) + p_new
    return p, p_new, l


def _nsa_decode_kernel(n_pages, pt_ref, *refs):
    ks_pages = refs[:n_pages]
    vs_pages = refs[n_pages:2 * n_pages]
    ck_pages = refs[2 * n_pages:3 * n_pages]
    cv_pages = refs[3 * n_pages:4 * n_pages]
    (q_ref, gt_ref, wk_ref, wv_ref, new_ref, cov_ref, exp_ref, w1k, w2k, pek, gk, w1v, w2v, pev,
     o_ref, scr) = refs[4 * n_pages:]
    d = HEAD_DIM
    t = n_pages * PAGE_SIZE
    n_c = t // CMP_STRIDE
    scr[t:, :] = jnp.zeros((scr.shape[0] - t, scr.shape[1]), F32)
    for p, pg in enumerate(ck_pages):
        scr[p * PAGE_SIZE:(p + 1) * PAGE_SIZE, :] = pg[0].T
    kc_raw = _compress_rows(scr, n_c, w1k, w2k, pek)
    kc = jnp.concatenate([_rms(kc_raw[:, sl], gk[:, sl]) for sl in (slice(0, d), slice(d, 2 * d))], axis=1)
    for p, pg in enumerate(cv_pages):
        scr[p * PAGE_SIZE:(p + 1) * PAGE_SIZE, :] = pg[0].T
    vc = _compress_rows(scr, n_c, w1v, w2v, pev)
    hrow = lax.broadcasted_iota(jnp.int32, (NSA_HEADS, 1), 0)
    grp0 = hrow < NSA_GROUP
    slope = jnp.exp2(-(hrow + 1).astype(F32))
    lane = lax.broadcasted_iota(jnp.int32, (1, LANES), 1)
    q64 = q_ref[0] * QK_SCALE
    q = jnp.where((lane >= d) == (hrow >= NSA_GROUP), jnp.concatenate([q64, q64], axis=1), 0.0)
    qb = q.astype(BF16)
    gates = jax.nn.sigmoid(gt_ref[0])
    new = new_ref[0]

    def own_group(o):
        return jnp.where(grp0, o[:, :d], o[:, d:])

    def new_row(r):
        return own_group(new[r:r + 1, :])

    c_end = lane * CMP_STRIDE + (CMP_BLOCK - 1)
    s = _nt_dot(qb, kc.astype(BF16)) - slope * (t - c_end).astype(F32)
    valid = c_end <= t
    sm = jnp.where(valid, s, NEG_INF)
    m = jnp.max(sm, axis=-1, keepdims=True)
    p = jnp.where(valid, jnp.exp(sm - m), 0.0)
    p_c = p / jnp.sum(p, axis=-1, keepdims=True)
    o_c = own_group(_dot(p_c.astype(BF16), vc.astype(BF16)))

    ps0 = jnp.sum(jnp.where(grp0, p_c, 0.0), axis=0, keepdims=True)
    ps1 = jnp.sum(jnp.where(grp0, 0.0, p_c), axis=0, keepdims=True)
    p_sum = jnp.where(grp0, ps0, ps1)
    imp = _dot(p_sum, cov_ref[...], precision=HIGHEST)
    own = t // SLC_BLOCK
    cur = jnp.where(lane < own, imp, -1.0)
    sel = _top_select(cur, lane, own, SLC_TOP)
    selk = _dot(sel.astype(BF16), exp_ref[...]) > 0.5

    ks = jnp.concatenate([pg[0] for pg in ks_pages], axis=1).astype(BF16)
    vs = jnp.concatenate([pg[0] for pg in vs_pages], axis=1).astype(BF16)
    pos = lax.broadcasted_iota(jnp.int32, (1, t), 1)
    s = _dot(qb, ks) - slope * (t - pos).astype(F32)
    s_new = jnp.sum(q * new[0:1, :], axis=-1, keepdims=True)
    p, p_new, l = _decode_softmax(s, selk, s_new)
    o_s = (own_group(_nt_dot(p.astype(BF16), vs)) + p_new * new_row(1)) / l

    wk = wk_ref[0].astype(BF16)
    wv = wv_ref[0].astype(BF16)
    w_n = wk.shape[1]
    wpos = (t - w_n) + lax.broadcasted_iota(jnp.int32, (1, w_n), 1)
    s = _dot(qb, wk) - slope * (t - wpos).astype(F32)
    s_new = jnp.sum(q * new[2:3, :], axis=-1, keepdims=True)
    p, p_new, l = _decode_softmax(s, (t - wpos) <= WINDOW, s_new)
    o_w = (own_group(_nt_dot(p.astype(BF16), wv)) + p_new * new_row(3)) / l

    o_ref[0] = gates[:, 0:1] * o_c + gates[:, 1:2] * o_s + gates[:, 2:3] * o_w


def _nsa_decode(page_table, ks_pool, vs_pool, ck_pool, cv_pool, q8, gates8, win_k, win_v, new_rows, wk, wv, kc_gain):
    b, n_pages = page_table.shape
    t = n_pages * PAGE_SIZE
    gd = NSA_KV_HEADS * HEAD_DIM
    page_specs = [pl.BlockSpec((1, gd, PAGE_SIZE), lambda i, pt, p=p: (pt[i * n_pages + p], 0, 0))
                  for p in range(n_pages)]
    per_b = lambda a: pl.BlockSpec((1,) + a.shape[1:], lambda i, pt: (i,) + (0,) * (a.ndim - 1))
    const = lambda a: pl.BlockSpec(a.shape, lambda i, pt: (0,) * a.ndim)
    consts = [_cover_matrix(), _expand_matrix(SLC_BLOCK, t)] + _cmp_consts(wk, wv, kc_gain)
    per_sample = [q8, gates8, win_k, win_v, new_rows]
    pools = [ks_pool, vs_pool, ck_pool, cv_pool]
    return pl.pallas_call(
        functools.partial(_nsa_decode_kernel, n_pages),
        grid_spec=pltpu.PrefetchScalarGridSpec(
            num_scalar_prefetch=1,
            grid=(b,),
            in_specs=page_specs * len(pools) + [per_b(a) for a in per_sample] + [const(a) for a in consts],
            out_specs=pl.BlockSpec((1, NSA_HEADS, HEAD_DIM), lambda i, pt: (i, 0, 0)),
            scratch_shapes=[pltpu.VMEM((t + CMP_STRIDE, gd), F32)],
        ),
        out_shape=jax.ShapeDtypeStruct((b, NSA_HEADS, HEAD_DIM), F32),
        compiler_params=_params("parallel"),
        name="nsa_decode",
    )(page_table.reshape(-1), *[pool for pool in pools for _ in range(n_pages)], *per_sample, *consts)


def _moba_decode_kernel(n_pages, pt_ref, *refs):
    k_pages = refs[:n_pages]
    v_pages = refs[n_pages:2 * n_pages]
    q_ref, new_ref, exp_ref, o_ref = refs[2 * n_pages:]
    d = HEAD_DIM
    hd = MOBA_HEADS * d
    t = n_pages * PAGE_SIZE
    ppb = MOBA_BLOCK // PAGE_SIZE
    n_b = n_pages // ppb
    hrow = lax.broadcasted_iota(jnp.int32, (MOBA_HEADS, 1), 0)
    slope = jnp.exp2(-(hrow + 1).astype(F32))
    lane = lax.broadcasted_iota(jnp.int32, (1, LANES), 1)
    head_of_lane = lax.broadcasted_iota(jnp.int32, (1, hd), 1) // d
    own_head = head_of_lane == hrow
    q = jnp.where(own_head, q_ref[0], 0.0)
    new = new_ref[0]

    k_mean = jnp.zeros((hd, LANES), F32)
    for n in range(n_b):
        blk = k_pages[n * ppb][0]
        for pp in range(1, ppb):
            blk = blk + k_pages[n * ppb + pp][0]
        col = jnp.sum(blk, axis=-1, keepdims=True) * (1.0 / MOBA_BLOCK)
        k_mean = jnp.where(lane == n, col, k_mean)
    route = _dot(q, k_mean, precision=HIGHEST)
    cur = jnp.where(lane < n_b, route, -2e38)
    sel = _top_select(cur, lane, n_b, MOBA_TOP)
    selk = _dot(sel.astype(BF16), exp_ref[...]) > 0.5

    k_all = jnp.concatenate([pg[0] for pg in k_pages], axis=1).astype(BF16)
    v_all = jnp.concatenate([pg[0] for pg in v_pages], axis=1).astype(BF16)
    qs = q * QK_SCALE
    pos = lax.broadcasted_iota(jnp.int32, (1, t), 1)
    s = _dot(qs.astype(BF16), k_all) - slope * (t - pos).astype(F32)
    s_new = jnp.sum(qs * new[0:1], axis=-1, keepdims=True)
    p, p_new, l = _decode_softmax(s, selk, s_new)
    o = (_nt_dot(p.astype(BF16), v_all) + p_new * new[1:2]) / l
    o_ref[0] = jnp.sum(jnp.where(own_head, o, 0.0), axis=0, keepdims=True)


def _moba_decode(page_table, k_pool, v_pool, q_rows, new_rows):
    b, n_pages = page_table.shape
    t = n_pages * PAGE_SIZE
    hd = MOBA_HEADS * HEAD_DIM
    page_specs = [pl.BlockSpec((1, hd, PAGE_SIZE), lambda i, pt, p=p: (pt[i * n_pages + p], 0, 0))
                  for p in range(n_pages)]
    per_b = lambda a: pl.BlockSpec((1,) + a.shape[1:], lambda i, pt: (i,) + (0,) * (a.ndim - 1))
    exp = _expand_matrix(MOBA_BLOCK, t)
    return pl.pallas_call(
        functools.partial(_moba_decode_kernel, n_pages),
        grid_spec=pltpu.PrefetchScalarGridSpec(
            num_scalar_prefetch=1,
            grid=(b,),
            in_specs=page_specs + page_specs + [per_b(q_rows), per_b(new_rows),
                                                pl.BlockSpec(exp.shape, lambda i, pt: (0, 0))],
            out_specs=pl.BlockSpec((1, 1, hd), lambda i, pt: (i, 0, 0)),
        ),
        out_shape=jax.ShapeDtypeStruct((b, 1, hd), F32),
        compiler_params=_params("parallel"),
        name="moba_decode",
    )(page_table.reshape(-1), *([k_pool] * n_pages), *([v_pool] * n_pages), q_rows, new_rows, exp)


def _diff_decode_kernel(lambda_init, n_step, n_total, pt_ref, *refs):
    k_pages = refs[:n_step]
    v_pages = refs[n_step:2 * n_step]
    q_ref, knew_ref, vnew_ref, lam_ref, sg_ref, o_ref, m_ref, l_ref, acc_ref = refs[2 * n_step:]
    step = pl.program_id(1)
    d = HEAD_DIM
    nh = DIFF_HEADS
    t = n_total * PAGE_SIZE
    row = lax.broadcasted_iota(jnp.int32, (2 * nh, 1), 0)
    lane = lax.broadcasted_iota(jnp.int32, (1, 2 * d), 1)
    slope = jnp.exp2(-(row % nh + 1).astype(F32))
    q8 = q_ref[0] * QK_SCALE
    q = jnp.concatenate([jnp.where(lane < d, q8, 0.0), jnp.where(lane >= d, q8, 0.0)], axis=0)

    @pl.when(step == 0)
    def _():
        m_ref[...] = jnp.full(m_ref.shape, NEG_INF, F32)
        l_ref[...] = jnp.zeros_like(l_ref)
        acc_ref[...] = jnp.zeros_like(acc_ref)

    k_all = jnp.concatenate([pg[0].reshape(PAGE_SIZE * nh, 2 * d) for pg in k_pages], axis=0).astype(BF16)
    v_all = jnp.concatenate([pg[0].reshape(PAGE_SIZE * nh, 2 * d) for pg in v_pages], axis=0).astype(BF16)
    n_k = n_step * PAGE_SIZE * nh
    col = lax.broadcasted_iota(jnp.int32, (1, n_k), 1)
    pos = step * (n_step * PAGE_SIZE) + col // nh
    mask = col % nh == row % nh
    s = _nt_dot(q.astype(BF16), k_all) - slope * (t - pos).astype(F32)
    s = jnp.where(mask, s, NEG_INF)
    m_old = m_ref[:, 0:1]
    m_new = jnp.maximum(m_old, jnp.max(s, axis=-1, keepdims=True))
    alpha = jnp.exp(m_old - m_new)
    p = jnp.where(mask, jnp.exp(s - m_new), 0.0)
    l_new = alpha * l_ref[:, 0:1] + jnp.sum(p, axis=-1, keepdims=True)
    acc_new = alpha * acc_ref[...] + _dot(p.astype(BF16), v_all)
    m_ref[...] = jnp.broadcast_to(m_new, m_ref.shape)
    l_ref[...] = jnp.broadcast_to(l_new, l_ref.shape)
    acc_ref[...] = acc_new

    @pl.when(step == pl.num_programs(1) - 1)
    def _():
        k_new = jnp.concatenate([knew_ref[0]] * 2, axis=0)
        v_new = jnp.concatenate([vnew_ref[0]] * 2, axis=0)
        s_new = jnp.sum(q * k_new, axis=-1, keepdims=True)
        m_fin = jnp.maximum(m_new, s_new)
        a = jnp.exp(m_new - m_fin)
        p_new = jnp.exp(s_new - m_fin)
        o = (a * acc_new + p_new * v_new) / (a * l_new + p_new)
        lam = _diff_lambda(lam_ref, lambda_init)
        attn = o[:nh] - lam * o[nh:]
        o_ref[0] = _rms(attn, sg_ref[...]) * (1.0 - lambda_init)


def _diff_decode(page_table, k_pool, v_pool, q_rows, k_new, v_new, lam_rows, subln_gain, lambda_init):
    b, n_pages = page_table.shape
    nh, d2 = DIFF_HEADS, 2 * HEAD_DIM
    n_step = n_pages // 2
    page_specs = [pl.BlockSpec((1, PAGE_SIZE, nh, d2),
                               lambda i, s, pt, p=p: (pt[i * n_pages + s * n_step + p], 0, 0, 0))
                  for p in range(n_step)]
    per_b = pl.BlockSpec((1, nh, d2), lambda i, s, pt: (i, 0, 0))
    const = lambda a: pl.BlockSpec(a.shape, lambda i, s, pt: (0,) * a.ndim)
    sg = subln_gain.reshape(1, d2)
    return pl.pallas_call(
        functools.partial(_diff_decode_kernel, lambda_init, n_step, n_pages),
        grid_spec=pltpu.PrefetchScalarGridSpec(
            num_scalar_prefetch=1,
            grid=(b, n_pages // n_step),
            in_specs=page_specs + page_specs + [per_b, per_b, per_b, const(lam_rows), const(sg)],
            out_specs=pl.BlockSpec((1, nh, d2), lambda i, s, pt: (i, 0, 0)),
            scratch_shapes=[pltpu.VMEM((2 * nh, LANES), F32), pltpu.VMEM((2 * nh, LANES), F32),
                            pltpu.VMEM((2 * nh, d2), F32)],
        ),
        out_shape=jax.ShapeDtypeStruct((b, nh, d2), F32),
        compiler_params=_params("parallel", "arbitrary"),
        name="diff_decode",
    )(page_table.reshape(-1), *([k_pool] * n_step), *([v_pool] * n_step), q_rows, k_new, v_new, lam_rows, sg)


def _even_weights(w_in, w_out, q_gain, ks_gain, kw_gain, mq_gain, mk_gain):
    d, g_n, ha, hb = HEAD_DIM, NSA_KV_HEADS, NSA_HEADS, MOBA_HEADS
    sizes = [ha * d] + [g_n * d] * 6 + [3 * ha, hb * d, hb * d, hb * d]
    offs = np.concatenate([[0], np.cumsum(sizes)])
    q_n, kc, vc, ks, vs, kw, vw, gates, q_m, k_m, v_m = [w_in[:, offs[i]:offs[i + 1]] for i in range(11)]
    gates = gates.reshape(-1, ha, 3).transpose(0, 2, 1).reshape(-1, 3 * ha)
    pad = jnp.zeros((w_in.shape[0], E_COLS - E_GT - 3 * ha), w_in.dtype)
    w = jnp.concatenate([q_n, ks, kw, q_m, k_m, kc, vc, vs, vw, v_m, gates, pad], axis=1)
    ones = jnp.ones((E_COLS - E_KC,), F32)
    col_gain = jnp.concatenate([jnp.tile(q_gain, ha), jnp.tile(ks_gain, g_n), jnp.tile(kw_gain, g_n),
                                jnp.tile(mq_gain, hb), jnp.tile(mk_gain, hb), ones])
    w_out = w_out.astype(BF16)
    return w, col_gain, w_out[:ha * d], w_out[ha * d:]


def _odd_weights(w_in, w_out, q_gain, k_gain):
    n_half = 2 * DIFF_HEADS
    col_gain = jnp.concatenate([jnp.tile(q_gain, n_half), jnp.tile(k_gain, n_half),
                                jnp.ones((O_COLS - O_V,), F32)])
    return w_in.astype(BF16), col_gain, w_out.astype(BF16)


def _trunk_prompt(x, prm):
    b, t, dm = x.shape
    d, g_n = HEAD_DIM, NSA_KV_HEADS
    gd = g_n * d
    n_pages = t // PAGE_SIZE
    x = x.reshape(b * t, dm)
    x = _ffn(x, prm["ffn1_norm"][0], prm["ffn1_w_in"][0], prm["ffn1_w_out"][0])
    pe = _proj(x, prm["mix_norm"][0], prm["even_w"], prm["even_gain"], E_FLAGS)
    col = lambda off, width: pe[:, off:off + width]
    kc_rows, vc_rows = col(E_KC, gd), col(E_VC, gd)
    kc, vc = _compress_prompt(pe, prm["cmp_k"], prm["cmp_v"], prm["nsa_kc_gain"], b, t)
    o_nsa = _nsa_attn(pe, kc, vc, b, t)
    o_moba = _moba_attn(pe, b, t)
    x = _oproj(x, [o_nsa, o_moba], [prm["even_wo_nsa"], prm["even_wo_moba"]])
    x = _ffn(x, prm["ffn2_norm"][0], prm["ffn2_w_in"][0], prm["ffn2_w_out"][0])
    n_keep = min(WINDOW, t)
    last = lambda a: a.reshape(b, t, g_n, d)[:, t - n_keep:]
    even = (kc_rows.reshape(b, t, g_n, d), vc_rows.reshape(b, t, g_n, d),
            col(E_KS, gd).reshape(b, t, g_n, d), col(E_VS, gd).reshape(b, t, g_n, d),
            col(E_KM, MOBA_HEADS * d).reshape(b, t, MOBA_HEADS, d),
            col(E_VM, MOBA_HEADS * d).reshape(b, t, MOBA_HEADS, d),
            last(col(E_KW, gd)), last(col(E_VW, gd)))
    x = _ffn(x, prm["ffn1_norm"][1], prm["ffn1_w_in"][1], prm["ffn1_w_out"][1])
    po = _proj(x, prm["mix_norm"][1], prm["odd_w"], prm["odd_gain"], O_FLAGS)
    o_diff = _diff_attn(po, prm["lam_rows"], prm["subln_gain"], prm["lambda_init"], b, t)
    x = _oproj(x, [o_diff], [prm["odd_wo"]])
    x = _ffn(x, prm["ffn2_norm"][1], prm["ffn2_w_in"][1], prm["ffn2_w_out"][1])
    odd = (po[:, O_K:O_V].reshape(b, t, DIFF_HEADS, 2 * d), po[:, O_V:].reshape(b, t, DIFF_HEADS, 2 * d))
    return x.reshape(b, t, dm), even, odd


def _trunk_sample(x, prm, past):
    b, t, dm = x.shape
    d, g_n = HEAD_DIM, NSA_KV_HEADS
    gd = g_n * d
    hm = MOBA_HEADS * d
    pt = past["page_table"]
    x = x.reshape(b, dm)
    x = _ffn(x, prm["ffn1_norm"][0], prm["ffn1_w_in32"][0], prm["ffn1_w_out32"][0])
    pe = _proj(x, prm["mix_norm"][0], prm["even_w32"], prm["even_gain"], E_FLAGS)
    col = lambda off, width: pe[:, off:off + width]
    pool_t = lambda c: jnp.transpose(c, (0, 2, 3, 1)).reshape(c.shape[0], -1, c.shape[1])
    q8 = col(E_QN, NSA_HEADS * d).reshape(b, NSA_HEADS, d)
    gates = col(E_GT, 3 * NSA_HEADS).reshape(b, 3, NSA_HEADS).transpose(0, 2, 1)
    gates8 = jnp.pad(gates, ((0, 0), (0, 0), (0, LANES - 3)))
    new_nsa = jnp.stack([col(E_KS, gd), col(E_VS, gd), col(E_KW, gd), col(E_VW, gd)], axis=1)
    new_nsa = jnp.pad(new_nsa, ((0, 0), (0, 4), (0, 0)))
    o_nsa = _nsa_decode(pt, pool_t(past["slc_k"]), pool_t(past["slc_v"]), pool_t(past["cmp_k"]),
                        pool_t(past["cmp_v"]), q8, gates8, pool_t(past["win_k"]), pool_t(past["win_v"]), new_nsa,
                        prm["cmp_k"], prm["cmp_v"], prm["nsa_kc_gain"]).reshape(b, NSA_HEADS * d)
    new_moba = jnp.pad(jnp.stack([col(E_KM, hm), col(E_VM, hm)], axis=1), ((0, 0), (0, 6), (0, 0)))
    o_moba = _moba_decode(pt, pool_t(past["moba_k"]), pool_t(past["moba_v"]),
                          col(E_QM, hm).reshape(b, 1, hm), new_moba).reshape(b, hm)
    x = _oproj(x, [o_nsa, o_moba], [prm["even_wo_nsa"], prm["even_wo_moba"]])
    x = _ffn(x, prm["ffn2_norm"][0], prm["ffn2_w_in"][0], prm["ffn2_w_out"][0])
    roll = lambda win, new: jnp.concatenate([win[:, 1:], new.reshape(b, 1, g_n, d)], axis=1)
    r4 = lambda a, h: a.reshape(b, 1, h, a.shape[-1] // h)
    even = (r4(col(E_KC, gd), g_n), r4(col(E_VC, gd), g_n), r4(col(E_KS, gd), g_n), r4(col(E_VS, gd), g_n),
            r4(col(E_KM, hm), MOBA_HEADS), r4(col(E_VM, hm), MOBA_HEADS),
            roll(past["win_k"], col(E_KW, gd)), roll(past["win_v"], col(E_VW, gd)))
    x = _ffn(x, prm["ffn1_norm"][1], prm["ffn1_w_in"][1], prm["ffn1_w_out"][1])
    po = _proj(x, prm["mix_norm"][1], prm["odd_w"], prm["odd_gain"], O_FLAGS)
    heads = lambda a: a.reshape(b, DIFF_HEADS, 2 * d)
    o_diff = _diff_decode(pt, past["diff_k"], past["diff_v"], heads(po[:, :O_K]), heads(po[:, O_K:O_V]),
                          heads(po[:, O_V:]), prm["lam_rows"], prm["subln_gain"],
                          prm["lambda_init"]).reshape(b, DIFF_HEADS * 2 * d)
    x = _oproj(x, [o_diff], [prm["odd_wo"]])
    x = _ffn(x, prm["ffn2_norm"][1], prm["ffn2_w_in"][1], prm["ffn2_w_out"][1])
    odd = (r4(po[:, O_K:O_V], DIFF_HEADS), r4(po[:, O_V:], DIFF_HEADS))
    return x.reshape(b, t, dm), even, odd


def kernel(x_prompt, x_sample, cache_nsa_cmp_k, cache_nsa_cmp_v, cache_nsa_slc_k, cache_nsa_slc_v,
           cache_moba_k, cache_moba_v, state_nsa_win_k, state_nsa_win_v, cache_diff_k, cache_diff_v,
           page_table, ffn1_norm, ffn1_w_in, ffn1_w_out, mix_norm, ffn2_norm, ffn2_w_in, ffn2_w_out,
           even_w_in, even_w_out, nsa_q_gain, nsa_kc_gain, nsa_ks_gain, nsa_kw_gain,
           cmp_k_pe, cmp_k_w1, cmp_k_w2, cmp_v_pe, cmp_v_w1, cmp_v_w2, moba_q_gain, moba_k_gain,
           odd_w_in, odd_w_out, diff_q_gain, diff_k_gain, diff_lq1, diff_lk1, diff_lq2, diff_lk2,
           diff_subln_gain):
    even_w, even_gain, wo_nsa, wo_moba = _even_weights(even_w_in[0], even_w_out[0], nsa_q_gain[0], nsa_ks_gain[0],
                                                       nsa_kw_gain[0], moba_q_gain[0], moba_k_gain[0])
    odd_w, odd_gain, odd_wo = _odd_weights(odd_w_in[0], odd_w_out[0], diff_q_gain[0], diff_k_gain[0])
    lam_rows = jnp.concatenate([diff_lq1, diff_lk1, diff_lq2, diff_lk2, jnp.zeros((4, HEAD_DIM), F32)], axis=0)
    prm = {
        "ffn1_norm": ffn1_norm, "ffn1_w_in": ffn1_w_in.astype(BF16), "ffn1_w_out": ffn1_w_out.astype(BF16),
        "ffn2_norm": ffn2_norm, "ffn2_w_in": ffn2_w_in.astype(BF16), "ffn2_w_out": ffn2_w_out.astype(BF16),
        "mix_norm": mix_norm,
        "even_w": even_w.astype(BF16), "even_gain": even_gain, "even_wo_nsa": wo_nsa, "even_wo_moba": wo_moba,
        "even_w32": even_w, "ffn1_w_in32": ffn1_w_in, "ffn1_w_out32": ffn1_w_out,
        "cmp_k": _cmp_weights(cmp_k_pe[0], cmp_k_w1[0], cmp_k_w2[0]),
        "cmp_v": _cmp_weights(cmp_v_pe[0], cmp_v_w1[0], cmp_v_w2[0]),
        "nsa_kc_gain": nsa_kc_gain[0],
        "odd_w": odd_w, "odd_gain": odd_gain, "odd_wo": odd_wo,
        "lam_rows": lam_rows, "subln_gain": diff_subln_gain[0],
        "lambda_init": 0.8 - 0.6 * math.exp(-0.3 * 1),
    }
    past = {
        "page_table": page_table,
        "cmp_k": cache_nsa_cmp_k[0], "cmp_v": cache_nsa_cmp_v[0],
        "slc_k": cache_nsa_slc_k[0], "slc_v": cache_nsa_slc_v[0],
        "moba_k": cache_moba_k[0], "moba_v": cache_moba_v[0],
        "win_k": state_nsa_win_k[0], "win_v": state_nsa_win_v[0],
        "diff_k": cache_diff_k[0], "diff_v": cache_diff_v[0],
    }
    y_p, even_p, odd_p = _trunk_prompt(x_prompt, prm)
    y_s, even_s, odd_s = _trunk_sample(x_sample, prm, past)
    outs = [y_p, y_s]
    for p_leaf, s_leaf in zip(even_p + odd_p, even_s + odd_s):
        outs += [p_leaf[None], s_leaf[None]]
    return tuple(outs)
```
